```python
import math
import jax
import jax.numpy as jnp
from jax import lax
import numpy as np

D_MODEL = 1024
BATCH = 16
SEQ = 2048
DEPTH = 2

CHUNK = 64
Q_BLOCK = 128
RMS_EPS = 1e-6

LRU_WIDTH = D_MODEL // 2
LRU_BLOCKS = 8
LRU_BLOCK_DIM = LRU_WIDTH // LRU_BLOCKS
CONV_WIDTH = 4
LRU_C = 8.0
MLSTM_WIDTH = D_MODEL // 2
MLSTM_HEADS = 4
MLSTM_HEAD_DIM = MLSTM_WIDTH // MLSTM_HEADS
AB_IN_COLS = 2 * LRU_WIDTH + 4 * MLSTM_WIDTH + 2 * MLSTM_HEADS
AB_OUT_WIDTH = LRU_WIDTH + MLSTM_WIDTH

DIFF_HEADS = 8
DIFF_V_DIM = D_MODEL // DIFF_HEADS
DIFF_QK_DIM = DIFF_V_DIM // 2
DIFF_WIDTH = DIFF_HEADS * DIFF_V_DIM

N_GROUPS = 4
EXPERTS_PER_GROUP = 8
N_EXPERTS = N_GROUPS * EXPERTS_PER_GROUP
TOP_K = 2
EXPERT_FF = D_MODEL // 2

N_AB_LAYERS = (DEPTH + 1) // 2
N_DIFF_LAYERS = DEPTH // 2

kernel_name = 'chunk_causal_hybrid_rglru_mlstm_diffattn_hmoe'


def rms_norm(x, g):
    xf = x.astype(jnp.float32)
    y = xf * lax.rsqrt(jnp.mean(xf * xf, axis=-1, keepdims=True) + RMS_EPS)
    return (y * g.astype(jnp.float32)).astype(x.dtype)


def causal_depthwise_conv(x, w, b):
    out = lax.conv_general_dilated(
        x, w[:, None, :].astype(x.dtype), window_strides=(1,),
        padding=[(CONV_WIDTH - 1, 0)], dimension_numbers=('NWC', 'WIO', 'NWC'),
        feature_group_count=x.shape[-1])
    return out + b.astype(x.dtype)


def block_diag_linear(x, w, b):
    xb = x.reshape(x.shape[:-1] + (LRU_BLOCKS, LRU_BLOCK_DIM))
    y = jnp.einsum('bsnd,nde->bsne', xb, w.astype(jnp.float32))
    return y.reshape(x.shape) + b.astype(jnp.float32)


def rg_lru(x, w_r, b_r, w_i, b_i, lam):
    xf = x.astype(jnp.float32)
    r = jax.nn.sigmoid(block_diag_linear(xf, w_r, b_r))
    i = jax.nn.sigmoid(block_diag_linear(xf, w_i, b_i))
    log_a = LRU_C * r * jax.nn.log_sigmoid(lam.astype(jnp.float32))
    a = jnp.exp(log_a)
    u = jnp.sqrt(-jnp.expm1(2.0 * log_a)) * (i * xf)

    def combine(left, right):
        a1, b1 = left
        a2, b2 = right
        return a1 * a2, a2 * b1 + b2

    _, h = lax.associative_scan(combine, (a, u), axis=1)
    return h.astype(x.dtype)


def mlstm_chunkwise(q, k, v, i_pre, f_pre):
    B, S, H, dh = q.shape
    nc = S // CHUNK

    def to_chunks(t):
        t = t.astype(jnp.float32).reshape((B, nc, CHUNK, H) + t.shape[3:])
        return jnp.moveaxis(jnp.moveaxis(t, 1, 0), 3, 2)

    qc = to_chunks(q)
    kc = to_chunks(k) * (dh ** -0.5)
    vc = to_chunks(v)
    ic = to_chunks(i_pre)
    bc = jnp.cumsum(jax.nn.log_sigmoid(to_chunks(f_pre)), axis=-1)
    causal = jnp.tril(jnp.ones((CHUNK, CHUNK), dtype=bool))

    def step(carry, inp):
        C, n, m = carry
        q_, k_, v_, b_, i_ = inp
        g_ = b_[..., -1]
        log_d = jnp.where(causal, b_[..., :, None] - b_[..., None, :] + i_[..., None, :], -jnp.inf)
        log_inter = b_ + m[..., None]
        m_t = jnp.maximum(log_inter, jnp.max(log_d, axis=-1))
        w_intra = jnp.exp(log_d - m_t[..., None])
        w_inter = jnp.exp(log_inter - m_t)
        s = jnp.einsum('bhld,bhsd->bhls', q_, k_) * w_intra
        num = (w_inter[..., None] * jnp.einsum('bhld,bhde->bhle', q_, C)
               + jnp.einsum('bhls,bhse->bhle', s, v_))
        den = w_inter * jnp.einsum('bhld,bhd->bhl', q_, n) + jnp.sum(s, axis=-1)
        h = num / jnp.maximum(jnp.abs(den), jnp.exp(-m_t))[..., None]
        log_w = g_[..., None] - b_ + i_
        m_new = jnp.maximum(g_ + m, jnp.max(log_w, axis=-1))
        w_state = jnp.exp(log_w - m_new[..., None])
        decay = jnp.exp(g_ + m - m_new)
        C = decay[..., None, None] * C + jnp.einsum('bhs,bhsd,bhse->bhde', w_state, k_, v_)
        n = decay[..., None] * n + jnp.einsum('bhs,bhsd->bhd', w_state, k_)
        return (C, n, m_new), h

    init = (jnp.zeros((B, H, dh, dh), jnp.float32),
            jnp.zeros((B, H, dh), jnp.float32),
            jnp.zeros((B, H), jnp.float32))
    _, h = lax.scan(step, init, (qc, kc, vc, bc, ic))
    return jnp.moveaxis(h, 0, 1).transpose(0, 1, 3, 2, 4).reshape(B, S, H, dh)


def hybrid_lru_mlstm_mixer(xn, w_in, w_out, conv_w, conv_b, w_r, b_r, w_i, b_i, lam,
                           mb_i, mb_f, mnorm_g):
    B, S, _ = xn.shape
    proj = xn @ w_in
    idx = np.cumsum([LRU_WIDTH, LRU_WIDTH, MLSTM_WIDTH, MLSTM_WIDTH, MLSTM_WIDTH,
                     MLSTM_WIDTH, MLSTM_HEADS]).tolist()
    xa, ga, q, k, v, o, ip, fp = jnp.split(proj, idx, axis=-1)
    xa = causal_depthwise_conv(xa, conv_w, conv_b)
    ya = rg_lru(xa, w_r, b_r, w_i, b_i, lam) * jax.nn.gelu(ga)
    hs = (B, S, MLSTM_HEADS, MLSTM_HEAD_DIM)
    hb = mlstm_chunkwise(q.reshape(hs), k.reshape(hs), v.reshape(hs),
                         ip + mb_i.astype(ip.dtype), fp + mb_f.astype(fp.dtype))
    hb = rms_norm(hb, mnorm_g) * jax.nn.sigmoid(o.astype(jnp.float32)).reshape(hs)
    yb = hb.reshape(B, S, MLSTM_WIDTH).astype(xn.dtype)
    return jnp.concatenate([ya, yb], axis=-1) @ w_out


def diff_attention(xn, w_qkv, w_o, lq1, lk1, lq2, lk2, norm_g, lambda_init):
    B, S, _ = xn.shape
    H = DIFF_HEADS
    q, k, v = jnp.split(xn @ w_qkv, [DIFF_WIDTH, 2 * DIFF_WIDTH], axis=-1)
    q = q.reshape(B, S, H, 2, DIFF_QK_DIM)
    k = k.reshape(B, S, H, 2, DIFF_QK_DIM)
    v = v.reshape(B, S, H, DIFF_V_DIM)
    f32 = jnp.float32
    lam = (jnp.exp(jnp.sum(lq1.astype(f32) * lk1.astype(f32)))
           - jnp.exp(jnp.sum(lq2.astype(f32) * lk2.astype(f32))) + lambda_init)
    slopes = jnp.exp2(-8.0 * jnp.arange(1, H + 1, dtype=f32) / H)
    scale = DIFF_QK_DIM ** -0.5
    n_blocks = S // Q_BLOCK
    q_blocks = jnp.moveaxis(q.reshape(B, n_blocks, Q_BLOCK, H, 2, DIFF_QK_DIM), 1, 0)
    k_pos = jnp.arange(S)

    def attend_block(args):
        q_blk, blk = args
        q_pos = blk * Q_BLOCK + jnp.arange(Q_BLOCK)
        scores = jnp.einsum('bqhcd,bkhcd->bhcqk', q_blk, k).astype(f32) * scale
        dist = jnp.abs(q_pos[:, None] - k_pos[None, :]).astype(f32)
        scores = scores - (slopes[:, None, None] * dist)[None, :, None]
        allowed = (k_pos[None, :] // CHUNK) <= (q_pos[:, None] // CHUNK)
        probs = jax.nn.softmax(jnp.where(allowed, scores, -jnp.inf), axis=-1)
        diff = probs[:, :, 0] - lam * probs[:, :, 1]
        return jnp.einsum('bhqk,bkhe->bqhe', diff.astype(v.dtype), v)

    out = lax.map(attend_block, (q_blocks, jnp.arange(n_blocks)))
    out = jnp.moveaxis(out, 0, 1).reshape(B, S, H, DIFF_V_DIM)
    out = rms_norm(out, norm_g) * (1.0 - lambda_init)
    return out.reshape(B, S, DIFF_WIDTH) @ w_o


def hierarchical_moe(xn, w_group, b_group, w_expert, b_expert, w_up, w_down):
    B, S, D = xn.shape
    t = xn.reshape(B * S, D)
    f32 = jnp.float32
    group_logits = (t @ w_group).astype(f32) + b_group.astype(f32)
    group_prob = jax.nn.softmax(group_logits, axis=-1)
    g_idx = jnp.argmax(group_logits, axis=-1)
    g_weight = jnp.take_along_axis(group_prob, g_idx[:, None], axis=-1)
    e_logits = ((t @ w_expert).astype(f32) + b_expert.astype(f32)).reshape(-1, N_GROUPS, EXPERTS_PER_GROUP)
    in_group = jnp.take_along_axis(e_logits, g_idx[:, None, None], axis=1)[:, 0]
    top_logits, top_idx = lax.top_k(in_group, TOP_K)
    top_w = jax.nn.softmax(top_logits, axis=-1) * g_weight
    expert_id = g_idx[:, None] * EXPERTS_PER_GROUP + top_idx
    gates = jnp.sum(jax.nn.one_hot(expert_id, N_EXPERTS, dtype=f32) * top_w[..., None], axis=1)
    y = jnp.zeros((B * S, D), f32)
    for e in range(N_EXPERTS):
        h = t @ w_up[e]
        h = jax.nn.silu(h[:, :EXPERT_FF]) * h[:, EXPERT_FF:]
        y = y + gates[:, e:e + 1] * (h @ w_down[e]).astype(f32)
    return y.astype(xn.dtype).reshape(B, S, D)


def setup_inputs(seed: int = 0) -> dict:
    key = jax.random.key(seed)
    ks = jax.random.split(key, 32)
    f32 = jnp.float32

    def nrm(k, shape, scale):
        return jax.random.normal(k, shape, f32) * scale

    u = jax.random.uniform(ks[10], (N_AB_LAYERS, LRU_WIDTH), f32, minval=0.9, maxval=0.999)
    a0 = u ** (1.0 / LRU_C)
    lru_lambda = jnp.log(a0) - jnp.log1p(-a0)
    return {
        'x': nrm(ks[0], (BATCH, SEQ, D_MODEL), 1.0),
        'norm_mix_g': 1.0 + nrm(ks[1], (DEPTH, D_MODEL), 0.05),
        'norm_ffn_g': 1.0 + nrm(ks[2], (DEPTH, D_MODEL), 0.05),
        'ab_w_in': nrm(ks[3], (N_AB_LAYERS, D_MODEL, AB_IN_COLS), D_MODEL ** -0.5),
        'ab_w_out': nrm(ks[4], (N_AB_LAYERS, AB_OUT_WIDTH, D_MODEL), AB_OUT_WIDTH ** -0.5),
        'lru_conv_w': nrm(ks[5], (N_AB_LAYERS, CONV_WIDTH, LRU_WIDTH), CONV_WIDTH ** -0.5),
        'lru_conv_b': nrm(ks[6], (N_AB_LAYERS, LRU_WIDTH), 0.02),
        'lru_w_r': nrm(ks[7], (N_AB_LAYERS, LRU_BLOCKS, LRU_BLOCK_DIM, LRU_BLOCK_DIM), LRU_BLOCK_DIM ** -0.5),
        'lru_b_r': nrm(ks[8], (N_AB_LAYERS, LRU_WIDTH), 0.02),
        'lru_w_i': nrm(ks[9], (N_AB_LAYERS, LRU_BLOCKS, LRU_BLOCK_DIM, LRU_BLOCK_DIM), LRU_BLOCK_DIM ** -0.5),
        'lru_b_i': nrm(ks[11], (N_AB_LAYERS, LRU_WIDTH), 0.02),
        'lru_lambda': lru_lambda,
        'mlstm_b_i': nrm(ks[12], (N_AB_LAYERS, MLSTM_HEADS), 0.1),
        'mlstm_b_f': jnp.linspace(3.0, 6.0, MLSTM_HEADS, dtype=f32)[None, :] + nrm(ks[13], (N_AB_LAYERS, MLSTM_HEADS), 0.1),
        'mlstm_norm_g': 1.0 + nrm(ks[14], (N_AB_LAYERS, MLSTM_HEADS, MLSTM_HEAD_DIM), 0.05),
        'diff_w_qkv': nrm(ks[15], (N_DIFF_LAYERS, D_MODEL, 3 * DIFF_WIDTH), D_MODEL ** -0.5),
        'diff_w_o': nrm(ks[16], (N_DIFF_LAYERS, DIFF_WIDTH, D_MODEL), DIFF_WIDTH ** -0.5),
        'diff_lambda_q1': nrm(ks[17], (N_DIFF_LAYERS, DIFF_QK_DIM), 0.1),
        'diff_lambda_k1': nrm(ks[18], (N_DIFF_LAYERS, DIFF_QK_DIM), 0.1),
        'diff_lambda_q2': nrm(ks[19], (N_DIFF_LAYERS, DIFF_QK_DIM), 0.1),
        'diff_lambda_k2': nrm(ks[20], (N_DIFF_LAYERS, DIFF_QK_DIM), 0.1),
        'diff_norm_g': 1.0 + nrm(ks[21], (N_DIFF_LAYERS, DIFF_HEADS, DIFF_V_DIM), 0.05),
        'moe_w_group': nrm(ks[22], (DEPTH, D_MODEL, N_GROUPS), D_MODEL ** -0.5),
        'moe_b_group': nrm(ks[23], (DEPTH, N_GROUPS), 0.01),
        'moe_w_expert': nrm(ks[24], (DEPTH, D_MODEL, N_EXPERTS), D_MODEL ** -0.5),
        'moe_b_expert': nrm(ks[25], (DEPTH, N_EXPERTS), 0.01),
        'moe_w_up': nrm(ks[26], (DEPTH, N_EXPERTS, D_MODEL, 2 * EXPERT_FF), D_MODEL ** -0.5),
        'moe_w_down': nrm(ks[27], (DEPTH, N_EXPERTS, EXPERT_FF, D_MODEL), EXPERT_FF ** -0.5),
        'final_norm_g': 1.0 + nrm(ks[28], (D_MODEL,), 0.05),
    }


def reference(x, norm_mix_g, norm_ffn_g, ab_w_in, ab_w_out, lru_conv_w, lru_conv_b, lru_w_r,
              lru_b_r, lru_w_i, lru_b_i, lru_lambda, mlstm_b_i, mlstm_b_f, mlstm_norm_g,
              diff_w_qkv, diff_w_o, diff_lambda_q1, diff_lambda_k1, diff_lambda_q2,
              diff_lambda_k2, diff_norm_g, moe_w_group, moe_b_group, moe_w_expert,
              moe_b_expert, moe_w_up, moe_w_down, final_norm_g):
    h = x
    for layer in range(DEPTH):
        hn = rms_norm(h, norm_mix_g[layer])
        j = layer // 2
        if layer % 2 == 0:
            h = h + hybrid_lru_mlstm_mixer(
                hn, ab_w_in[j], ab_w_out[j], lru_conv_w[j], lru_conv_b[j], lru_w_r[j],
                lru_b_r[j], lru_w_i[j], lru_b_i[j], lru_lambda[j], mlstm_b_i[j],
                mlstm_b_f[j], mlstm_norm_g[j])
        else:
            lambda_init = 0.8 - 0.6 * math.exp(-0.3 * layer)
            h = h + diff_attention(
                hn, diff_w_qkv[j], diff_w_o[j], diff_lambda_q1[j], diff_lambda_k1[j],
                diff_lambda_q2[j], diff_lambda_k2[j], diff_norm_g[j], lambda_init)
        h = h + hierarchical_moe(rms_norm(h, norm_ffn_g[layer]), moe_w_group[layer],
                                 moe_b_group[layer], moe_w_expert[layer], moe_b_expert[layer],
                                 moe_w_up[layer], moe_w_down[layer])
    return rms_norm(h, final_norm_g)
```

```python
import functools
import math

import jax
import jax.numpy as jnp
from jax import lax
from jax.experimental import pallas as pl
from jax.experimental.pallas import tpu as pltpu

F32 = jnp.float32
BF16 = jnp.bfloat16
I32 = jnp.int32

RMS_EPS = 1e-6
CHUNK = 64
LRU_C = 8.0
N_GROUPS = 4
EXPERTS_PER_GROUP = 8
N_EXPERTS = N_GROUPS * EXPERTS_PER_GROUP
TOP_K = 2

LANES = 128
SUBLANES = 8
VMEM_LIMIT = 56 * 1024 * 1024

TM = 512
TS_LRU = 512
TQ = 256
TK = 256
TM_MOE = 512

NEG_INF = float("-inf")


def _cparams(sem):
    return pltpu.CompilerParams(dimension_semantics=sem, vmem_limit_bytes=VMEM_LIMIT)


def _dot(a, b, precision=None):
    return jnp.dot(a, b, preferred_element_type=F32, precision=precision)


def _dot_nt(a, b):
    return lax.dot_general(a, b, (((1,), (1,)), ((), ())), preferred_element_type=F32)


def _rms(x, g):
    ms = jnp.mean(x * x, axis=-1, keepdims=True)
    return x * lax.rsqrt(ms + RMS_EPS) * g


def _log_sigmoid(x):
    return jnp.minimum(x, 0.0) - jnp.log1p(jnp.exp(-jnp.abs(x)))


def _sigmoid(x):
    return 1.0 / (1.0 + jnp.exp(-x))


def _inproj0_body(x_ref, g_ref, w_ref, wg_ref, bg_ref, pf_ref, qkvo_ref, gate_ref):
    d = x_ref.shape[1]
    xb = _rms(x_ref[...], g_ref[...]).astype(BF16)
    pf_ref[...] = _dot(xb, w_ref[:, 0:d])
    for c in range(qkvo_ref.shape[1] // d):
        qkvo_ref[:, c * d:(c + 1) * d] = _dot(xb, w_ref[:, (c + 1) * d:(c + 2) * d]).astype(BF16)
    gate_ref[...] = _dot(xb, wg_ref[...]) + bg_ref[...]


def _inproj0(x2, g, w_main, w_gate, b_gate):
    t, d = x2.shape
    n_main = w_main.shape[1]
    grid = (t // TM,)
    return pl.pallas_call(
        _inproj0_body,
        grid=grid,
        in_specs=[
            pl.BlockSpec((TM, d), lambda i: (i, 0)),
            pl.BlockSpec((1, d), lambda i: (0, 0)),
            pl.BlockSpec((d, n_main), lambda i: (0, 0)),
            pl.BlockSpec((d, LANES), lambda i: (0, 0)),
            pl.BlockSpec((1, LANES), lambda i: (0, 0)),
        ],
        out_specs=[
            pl.BlockSpec((TM, d), lambda i: (i, 0)),
            pl.BlockSpec((TM, n_main - d), lambda i: (i, 0)),
            pl.BlockSpec((TM, LANES), lambda i: (i, 0)),
        ],
        out_shape=[
            jax.ShapeDtypeStruct((t, d), F32),
            jax.ShapeDtypeStruct((t, n_main - d), BF16),
            jax.ShapeDtypeStruct((t, LANES), F32),
        ],
        compiler_params=_cparams(("parallel",)),
        name="inproj0",
    )(x2, g, w_main, w_gate, b_gate)


def _lru_body(pf_ref, cw_ref, cb_ref, wri_ref, bri_ref, lam_ref, ya_ref, xbuf, a_s, u_s, hcar):
    ts = pf_ref.shape[0]
    c = ya_ref.shape[1]
    conv_w = cw_ref.shape[0]
    halo = SUBLANES

    @pl.when(pl.program_id(1) == 0)
    def _():
        xbuf[0:halo, :] = jnp.zeros((halo, c), F32)
        hcar[...] = jnp.zeros_like(hcar)

    xbuf[halo:, :] = pf_ref[:, 0:c]
    xc = cb_ref[...] + cw_ref[0:1, :] * xbuf[pl.ds(halo - conv_w + 1, ts), :]
    for j in range(1, conv_w):
        xc = xc + cw_ref[j:j + 1, :] * xbuf[pl.ds(halo - conv_w + 1 + j, ts), :]
    xbuf[0:halo, :] = xbuf[ts:ts + halo, :]

    ri = _dot(xc.astype(BF16), wri_ref[...]) + bri_ref[...]
    r = _sigmoid(ri[:, 0:c])
    i = _sigmoid(ri[:, c:2 * c])
    log_a = LRU_C * r * _log_sigmoid(lam_ref[...])
    a = jnp.exp(log_a)
    a_s[...] = a
    u_s[...] = jnp.sqrt(-jnp.tanh(log_a) * (a * a + 1.0)) * (i * xc)

    row = lax.broadcasted_iota(I32, (SUBLANES, c), 0)

    def blk(j, carry):
        r0 = pl.multiple_of(j * SUBLANES, SUBLANES)
        a = a_s[pl.ds(r0, SUBLANES), :]
        u = u_s[pl.ds(r0, SUBLANES), :]
        for sh in (1, 2, 4):
            a_sh = jnp.where(row >= sh, pltpu.roll(a, sh, 0), 1.0)
            u_sh = jnp.where(row >= sh, pltpu.roll(u, sh, 0), 0.0)
            u = a * u_sh + u
            a = a * a_sh
        h = a * carry + u
        u_s[pl.ds(r0, SUBLANES), :] = h
        return h[SUBLANES - 1:SUBLANES, :]

    hcar[...] = lax.fori_loop(0, ts // SUBLANES, blk, hcar[...], unroll=2)
    ya_ref[...] = (u_s[...] * jax.nn.gelu(pf_ref[:, c:2 * c])).astype(BF16)


def _lru(pf, conv_w, conv_b, w_ri, b_ri, lam, batch, seq):
    t = pf.shape[0]
    c = lam.shape[1]
    ns = seq // TS_LRU
    return pl.pallas_call(
        _lru_body,
        grid=(batch, ns),
        in_specs=[
            pl.BlockSpec((TS_LRU, 2 * c), lambda b, s: (b * ns + s, 0)),
            pl.BlockSpec(conv_w.shape, lambda b, s: (0, 0)),
            pl.BlockSpec((1, c), lambda b, s: (0, 0)),
            pl.BlockSpec((c, 2 * c), lambda b, s: (0, 0)),
            pl.BlockSpec((1, 2 * c), lambda b, s: (0, 0)),
            pl.BlockSpec((1, c), lambda b, s: (0, 0)),
        ],
        out_specs=pl.BlockSpec((TS_LRU, c), lambda b, s: (b * ns + s, 0)),
        out_shape=jax.ShapeDtypeStruct((t, c), BF16),
        scratch_shapes=[
            pltpu.VMEM((TS_LRU + SUBLANES, c), F32),
            pltpu.VMEM((TS_LRU, c), F32),
            pltpu.VMEM((TS_LRU, c), F32),
            pltpu.VMEM((1, c), F32),
        ],
        compiler_params=_cparams(("parallel", "arbitrary")),
        name="rglru",
    )(pf, conv_w, conv_b, w_ri, b_ri, lam)


def _mlstm_body(qkvo_ref, gate_ref, grow_ref, mg_ref, yb_ref, c_s, m_s, *, heads, dh):
    seq = qkvo_ref.shape[0]
    nc = seq // CHUNK
    width = heads * dh
    scale = dh ** -0.5
    hi = lax.Precision.HIGHEST

    c_s[...] = jnp.zeros_like(c_s)
    m_s[...] = jnp.zeros_like(m_s)

    rr = lax.broadcasted_iota(I32, (CHUNK, CHUNK), 0)
    cc = lax.broadcasted_iota(I32, (CHUNK, CHUNK), 1)
    causal = cc <= rr
    tril = causal.astype(F32)
    triu = (rr <= cc).astype(F32)
    ones_col = (lax.broadcasted_iota(I32, (CHUNK, dh), 1) == 0).astype(BF16)

    def step(ci, carry):
        r0 = pl.multiple_of(ci * CHUNK, CHUNK)
        gcol = gate_ref[pl.ds(r0, CHUNK), :]
        grow = grow_ref[ci]
        bcol = _dot(tril, _log_sigmoid(gcol), precision=hi)
        brow = _dot(_log_sigmoid(grow), triu, precision=hi)
        for h in range(heads):
            i_col = gcol[:, h:h + 1]
            b_col = bcol[:, heads + h:heads + h + 1]
            i_row = grow[h:h + 1, :]
            b_row = brow[heads + h:heads + h + 1, :]
            q = qkvo_ref[pl.ds(r0, CHUNK), h * dh:(h + 1) * dh]
            k = qkvo_ref[pl.ds(r0, CHUNK), width + h * dh:width + (h + 1) * dh]
            v = qkvo_ref[pl.ds(r0, CHUNK), 2 * width + h * dh:2 * width + (h + 1) * dh]
            o = qkvo_ref[pl.ds(r0, CHUNK), 3 * width + h * dh:3 * width + (h + 1) * dh]
            m_prev = m_s[h:h + 1, 0:1]
            cst = c_s[h]

            log_d = jnp.where(causal, b_col - b_row + i_row, NEG_INF)
            log_inter = b_col + m_prev
            m_t = jnp.maximum(log_inter, jnp.max(log_d, axis=-1, keepdims=True))
            w_intra = jnp.exp(log_d - m_t)
            w_inter = jnp.exp(log_inter - m_t)
            s = _dot_nt(q, k) * scale * w_intra
            v_aug = jnp.concatenate([v, ones_col], axis=1)
            res = w_inter * _dot(q, cst.astype(BF16)) + _dot(s.astype(BF16), v_aug)
            num = res[:, 0:dh]
            den = res[:, dh:dh + 1]
            hh = num / jnp.maximum(jnp.abs(den), jnp.exp(-m_t))
            y = _rms(hh, mg_ref[:, h * dh:(h + 1) * dh]) * _sigmoid(o.astype(F32))
            yb_ref[pl.ds(r0, CHUNK), h * dh:(h + 1) * dh] = y.astype(BF16)

            g = b_col[CHUNK - 1:CHUNK, :]
            log_w = g - b_col + i_col
            m_new = jnp.maximum(g + m_prev, jnp.max(log_w, axis=0, keepdims=True))
            w_state = jnp.exp(log_w - m_new) * scale
            decay = jnp.exp(g + m_prev - m_new)
            wv = (v_aug.astype(F32) * w_state).astype(BF16)
            upd = _dot(k.astype(F32).T.astype(BF16), wv)
            c_s[h] = decay * cst + upd
            m_s[h:h + 1, 0:1] = m_new
        return carry

    lax.fori_loop(0, nc, step, 0)


def _mlstm(qkvo, gate, grow, mnorm_g, batch, seq, heads, dh):
    t = qkvo.shape[0]
    width = heads * dh
    nc = seq // CHUNK
    return pl.pallas_call(
        functools.partial(_mlstm_body, heads=heads, dh=dh),
        grid=(batch,),
        in_specs=[
            pl.BlockSpec((seq, 4 * width), lambda b: (b, 0)),
            pl.BlockSpec((seq, LANES), lambda b: (b, 0)),
            pl.BlockSpec((nc, SUBLANES, CHUNK), lambda b: (b, 0, 0)),
            pl.BlockSpec((1, width), lambda b: (0, 0)),
        ],
        out_specs=pl.BlockSpec((seq, width), lambda b: (b, 0)),
        out_shape=jax.ShapeDtypeStruct((t, width), BF16),
        scratch_shapes=[
            pltpu.VMEM((heads, dh, 2 * dh), F32),
            pltpu.VMEM((SUBLANES, LANES), F32),
        ],
        compiler_params=_cparams(("parallel",)),
        name="mlstm",
    )(qkvo, gate, grow, mnorm_g)


def _route(logits_t):
    tm = logits_t.shape[1]
    row = lax.broadcasted_iota(I32, (SUBLANES, tm), 0)
    rowf = row.astype(F32)
    none = float(SUBLANES)
    gl = jnp.where(row < N_GROUPS, logits_t[0:SUBLANES], NEG_INF)
    gmax = jnp.max(gl, axis=0, keepdims=True)
    gidx = jnp.min(jnp.where(gl == gmax, rowf, none), axis=0, keepdims=True)
    gweight = 1.0 / jnp.sum(jnp.exp(gl - gmax), axis=0, keepdims=True)
    el = logits_t[SUBLANES:2 * SUBLANES]
    for g in range(1, N_GROUPS):
        el = jnp.where(gidx == float(g), logits_t[(g + 1) * SUBLANES:(g + 2) * SUBLANES], el)
    v1 = jnp.max(el, axis=0, keepdims=True)
    i1 = jnp.min(jnp.where(el == v1, rowf, none), axis=0, keepdims=True)
    el2 = jnp.where(rowf == i1, NEG_INF, el)
    v2 = jnp.max(el2, axis=0, keepdims=True)
    i2 = jnp.min(jnp.where(el2 == v2, rowf, none), axis=0, keepdims=True)
    t = jnp.exp(v2 - v1)
    w1 = gweight / (1.0 + t)
    w2 = gweight * t / (1.0 + t)
    e1 = (gidx * EXPERTS_PER_GROUP + i1).astype(I32)
    e2 = (gidx * EXPERTS_PER_GROUP + i2).astype(I32)
    ids = jnp.where(row == 0, e1, jnp.where(row == 1, e2, 0))
    wts = jnp.where(row == 0, w1, jnp.where(row == 1, w2, 0.0))
    return ids, wts


def _outproj_router_body(*refs, nact):
    h_ref = refs[0]
    act_refs = refs[1:1 + nact]
    w_refs = refs[1 + nact:1 + 2 * nact]
    g_ref, wr_ref, br_ref, hn_ref, xn_ref, ri_ref, rw_ref = refs[1 + 2 * nact:]
    acc = h_ref[...]
    for a_ref, w_ref in zip(act_refs, w_refs):
        acc = acc + _dot(a_ref[...], w_ref[...])
    hn_ref[...] = acc
    xn = _rms(acc, g_ref[...])
    xn_ref[...] = xn
    logits = _dot(xn, wr_ref[...], precision=lax.Precision.HIGHEST) + br_ref[...]
    ids, wts = _route(logits.T)
    ri_ref[...] = ids
    rw_ref[...] = wts


def _outproj_router(h, acts, ws, g, wr, br):
    t, d = h.shape
    nact = len(acts)
    row_spec = pl.BlockSpec((TM, d), lambda i: (i, 0))
    in_specs = [row_spec]
    in_specs += [pl.BlockSpec((TM, a.shape[1]), lambda i: (i, 0)) for a in acts]
    in_specs += [pl.BlockSpec(w.shape, lambda i: (0, 0)) for w in ws]
    in_specs += [
        pl.BlockSpec((1, d), lambda i: (0, 0)),
        pl.BlockSpec((d, LANES), lambda i: (0, 0)),
        pl.BlockSpec((1, LANES), lambda i: (0, 0)),
    ]
    return pl.pallas_call(
        functools.partial(_outproj_router_body, nact=nact),
        grid=(t // TM,),
        in_specs=in_specs,
        out_specs=[
            row_spec,
            row_spec,
            pl.BlockSpec((SUBLANES, TM), lambda i: (0, i)),
            pl.BlockSpec((SUBLANES, TM), lambda i: (0, i)),
        ],
        out_shape=[
            jax.ShapeDtypeStruct((t, d), F32),
            jax.ShapeDtypeStruct((t, d), F32),
            jax.ShapeDtypeStruct((SUBLANES, t), I32),
            jax.ShapeDtypeStruct((SUBLANES, t), F32),
        ],
        compiler_params=_cparams(("parallel",)),
        name="outproj_router",
    )(h, *acts, *ws, g, wr, br)


def _moe_body(te_ref, nt_ref, tok_ref, dst_ref, gate_ref, xn_hbm, wup_ref, wdn_ref, y_hbm,
              tok_s, dst_s, xbuf, ybuf, sem_i, sem_g, sem_s):
    i = pl.program_id(0)
    tm, d = xbuf.shape
    ff = wdn_ref.shape[0]

    @pl.when(i == 0)
    def _():
        ybuf[...] = jnp.zeros_like(ybuf)
        pad = pltpu.make_async_copy(ybuf, y_hbm.at[pl.ds(y_hbm.shape[0] - tm, tm), :], sem_s.at[0])
        pad.start()
        pad.wait()

    @pl.when(i < nt_ref[0])
    def _():
        ci = pltpu.make_async_copy(tok_ref, tok_s, sem_i.at[0])
        cd = pltpu.make_async_copy(dst_ref, dst_s, sem_i.at[1])
        ci.start()
        cd.start()
        ci.wait()
        cd.wait()

        def gather_copy(r):
            return pltpu.make_async_copy(xn_hbm.at[pl.ds(tok_s[0, 0, r], 1), :], xbuf.at[pl.ds(r, 1), :], sem_g.at[0])

        def scatter_copy(r):
            return pltpu.make_async_copy(ybuf.at[pl.ds(r, 1), :], y_hbm.at[pl.ds(dst_s[0, 0, r], 1), :], sem_s.at[0])

        def start_g(r, c):
            gather_copy(r).start()
            return c

        def wait_g(r, c):
            gather_copy(r).wait()
            return c

        lax.fori_loop(0, tm, start_g, 0, unroll=8)
        lax.fori_loop(0, tm, wait_g, 0, unroll=8)

        gate_t = jnp.broadcast_to(gate_ref[0], (LANES, tm)).T
        hcat = _dot(xbuf[...].astype(BF16), wup_ref[...])
        hh = jax.nn.silu(hcat[:, 0:ff]) * hcat[:, ff:2 * ff] * gate_t[:, 0:1]
        ybuf[...] = _dot(hh.astype(BF16), wdn_ref[...])

        def start_s(r, c):
            scatter_copy(r).start()
            return c

        def wait_s(r, c):
            scatter_copy(r).wait()
            return c

        lax.fori_loop(0, tm, start_s, 0, unroll=8)
        lax.fori_loop(0, tm, wait_s, 0, unroll=8)


def _moe(xn, tile_expert, num_tiles, tok, dst, gate, w_up, w_down, n_rows_out):
    t, d = xn.shape
    ntiles, _, tm = tok.shape
    ff = w_down.shape[1]
    tile_spec = pl.BlockSpec((1, 1, tm), lambda i, te, nt: (i, 0, 0))
    grid_spec = pltpu.PrefetchScalarGridSpec(
        num_scalar_prefetch=2,
        grid=(ntiles,),
        in_specs=[
            tile_spec,
            tile_spec,
            tile_spec,
            pl.BlockSpec(memory_space=pl.ANY),
            pl.BlockSpec((None, d, 2 * ff), lambda i, te, nt: (te[i], 0, 0)),
            pl.BlockSpec((None, ff, d), lambda i, te, nt: (te[i], 0, 0)),
        ],
        out_specs=pl.BlockSpec(memory_space=pl.ANY),
        scratch_shapes=[
            pltpu.SMEM((1, 1, tm), I32),
            pltpu.SMEM((1, 1, tm), I32),
            pltpu.VMEM((tm, d), F32),
            pltpu.VMEM((tm, d), F32),
            pltpu.SemaphoreType.DMA((2,)),
            pltpu.SemaphoreType.DMA((1,)),
            pltpu.SemaphoreType.DMA((1,)),
        ],
    )
    return pl.pallas_call(
        _moe_body,
        grid_spec=grid_spec,
        out_shape=jax.ShapeDtypeStruct((n_rows_out, d), F32),
        compiler_params=_cparams(("arbitrary",)),
        name="moe_experts",
    )(tile_expert, num_tiles, tok, dst, gate, xn, w_up, w_down)


def _dispatch(ri, rw, t):
    tm = TM_MOE
    n_assign = TOP_K * t
    ntiles = n_assign // tm + N_EXPERTS
    e_flat = ri[0:TOP_K].reshape(n_assign)
    w_flat = rw[0:TOP_K].reshape(n_assign)
    order = jnp.argsort(e_flat, stable=True).astype(I32)
    counts = jnp.sum((e_flat[:, None] == jnp.arange(N_EXPERTS, dtype=I32)[None, :]).astype(I32), axis=0)
    tiles_per = (counts + tm - 1) // tm
    tile_end = jnp.cumsum(tiles_per)
    num_tiles = tile_end[-1:].astype(I32)
    tile_ids = jnp.arange(ntiles, dtype=I32)
    tile_expert = jnp.minimum(jnp.sum((tile_ids[:, None] >= tile_end[None, :]).astype(I32), axis=1), N_EXPERTS - 1)
    last_valid = jnp.take(tile_expert, jnp.maximum(num_tiles[0] - 1, 0))
    tile_expert = jnp.where(tile_ids < num_tiles[0], tile_expert, last_valid).astype(I32)
    start = jnp.cumsum(counts) - counts
    tile_start = (tile_end - tiles_per) * tm
    pos = jnp.arange(ntiles * tm, dtype=I32)
    e_pos = jnp.repeat(tile_expert, tm)
    off = pos - jnp.take(tile_start, e_pos)
    valid = (off < jnp.take(counts, e_pos)) & (pos < num_tiles[0] * tm)
    a = jnp.take(order, jnp.clip(jnp.take(start, e_pos) + off, 0, n_assign - 1))
    tok = jnp.where(valid, a % t, 0).astype(I32).reshape(ntiles, 1, tm)
    dst = jnp.where(valid, a, n_assign + pos % tm).astype(I32).reshape(ntiles, 1, tm)
    gate = jnp.where(valid, jnp.take(w_flat, a), 0.0).reshape(ntiles, 1, tm)
    return tile_expert, num_tiles, tok, dst, gate


def _inproj1_body(h_ref, y0_ref, y1_ref, g_ref, w_ref, hn_ref, qkv_ref):
    d = h_ref.shape[1]
    h = h_ref[...] + y0_ref[...] + y1_ref[...]
    hn_ref[...] = h
    xb = _rms(h, g_ref[...]).astype(BF16)
    for c in range(w_ref.shape[1] // d):
        qkv_ref[:, c * d:(c + 1) * d] = _dot(xb, w_ref[:, c * d:(c + 1) * d]).astype(BF16)


def _inproj1(h, y2, g, w):
    t, d = h.shape
    n = w.shape[1]
    nb = t // TM
    row_spec = pl.BlockSpec((TM, d), lambda i: (i, 0))
    return pl.pallas_call(
        _inproj1_body,
        grid=(nb,),
        in_specs=[
            row_spec,
            row_spec,
            pl.BlockSpec((TM, d), lambda i: (i + nb, 0)),
            pl.BlockSpec((1, d), lambda i: (0, 0)),
            pl.BlockSpec((d, n), lambda i: (0, 0)),
        ],
        out_specs=[row_spec, pl.BlockSpec((TM, n), lambda i: (i, 0))],
        out_shape=[jax.ShapeDtypeStruct((t, d), F32), jax.ShapeDtypeStruct((t, n), BF16)],
        compiler_params=_cparams(("parallel",)),
        name="inproj1",
    )(h, y2, y2, g, w)


def _attn_body(q_ref, k_ref, v_ref, lq1_ref, lk1_ref, lq2_ref, lk2_ref, ng_ref, o_ref,
               qs, bias_s, m_s, l_s, acc_s, *, heads, lambda_init):
    h = pl.program_id(1)
    qi = pl.program_id(2)
    tq, dv = q_ref.shape
    dqk = dv // 2
    slope = jnp.exp2(jnp.full((1, 1), -8.0 / heads, F32) * (h + 1).astype(F32))
    scale = dqk ** -0.5

    lane = lax.broadcasted_iota(I32, (tq, dv), 1)
    q = q_ref[...].astype(F32) * scale
    qs[0:tq, :] = jnp.where(lane < dqk, q, 0.0).astype(BF16)
    qs[tq:2 * tq, :] = jnp.where(lane >= dqk, q, 0.0).astype(BF16)

    rowp = lax.broadcasted_iota(I32, (2 * tq, TK), 0) % tq
    colp = lax.broadcasted_iota(I32, (2 * tq, TK), 1)
    rel = (rowp - colp).astype(F32)
    bias_s[...] = -slope * rel

    m_s[...] = jnp.full_like(m_s, NEG_INF)
    l_s[...] = jnp.zeros_like(l_s)
    acc_s[...] = jnp.zeros_like(acc_s)

    def update(s, v):
        m_old = m_s[...]
        m_new = jnp.maximum(m_old, jnp.max(s, axis=-1, keepdims=True))
        alpha = jnp.exp(m_old - m_new)
        p = jnp.exp(s - m_new)
        l_s[...] = alpha * l_s[...] + jnp.sum(p, axis=-1, keepdims=True)
        acc_s[...] = alpha * acc_s[...] + _dot(p.astype(BF16), v)
        m_s[...] = m_new

    def off_diag(j, carry):
        k0 = pl.multiple_of(j * TK, TK)
        k = k_ref[pl.ds(k0, TK), :]
        v = v_ref[pl.ds(k0, TK), :]
        s = _dot_nt(qs[...], k) + bias_s[...] - slope * ((qi - j) * TK).astype(F32)
        update(s, v)
        return carry

    lax.fori_loop(0, qi, off_diag, 0)

    k0 = pl.multiple_of(qi * TK, TK)
    k = k_ref[pl.ds(k0, TK), :]
    v = v_ref[pl.ds(k0, TK), :]
    allowed = (colp // CHUNK) <= (rowp // CHUNK)
    s = jnp.where(allowed, _dot_nt(qs[...], k) - slope * jnp.abs(rel), NEG_INF)
    update(s, v)

    lam = (jnp.exp(jnp.sum(lq1_ref[...] * lk1_ref[...], axis=-1, keepdims=True))
           - jnp.exp(jnp.sum(lq2_ref[...] * lk2_ref[...], axis=-1, keepdims=True)) + lambda_init)
    out = acc_s[0:tq, :] / l_s[0:tq, :] - lam * (acc_s[tq:2 * tq, :] / l_s[tq:2 * tq, :])
    o_ref[...] = (_rms(out, ng_ref[pl.ds(h, 1), :]) * (1.0 - lambda_init)).astype(BF16)


def _attn(qkv, lq1, lk1, lq2, lk2, norm_g, batch, seq, heads, dv, lambda_init):
    t = qkv.shape[0]
    nq = seq // TQ
    dqk = dv // 2
    small = pl.BlockSpec((1, dqk), lambda b, h, i: (0, 0))
    return pl.pallas_call(
        functools.partial(_attn_body, heads=heads, lambda_init=lambda_init),
        grid=(batch, heads, nq),
        in_specs=[
            pl.BlockSpec((TQ, dv), lambda b, h, i: (b * nq + i, h)),
            pl.BlockSpec((seq, dv), lambda b, h, i: (b, heads + h)),
            pl.BlockSpec((seq, dv), lambda b, h, i: (b, 2 * heads + h)),
            small, small, small, small,
            pl.BlockSpec((heads, dv), lambda b, h, i: (0, 0)),
        ],
        out_specs=pl.BlockSpec((TQ, dv), lambda b, h, i: (b * nq + i, h)),
        out_shape=jax.ShapeDtypeStruct((t, heads * dv), BF16),
        scratch_shapes=[
            pltpu.VMEM((2 * TQ, dv), BF16),
            pltpu.VMEM((2 * TQ, TK), F32),
            pltpu.VMEM((2 * TQ, 1), F32),
            pltpu.VMEM((2 * TQ, 1), F32),
            pltpu.VMEM((2 * TQ, dv), F32),
        ],
        compiler_params=_cparams(("parallel", "parallel", "arbitrary")),
        name="diff_attn",
    )(qkv, qkv, qkv, lq1, lk1, lq2, lk2, norm_g)


def _final_body(h_ref, y0_ref, y1_ref, g_ref, o_ref):
    o_ref[...] = _rms(h_ref[...] + y0_ref[...] + y1_ref[...], g_ref[...])


def _final(h, y2, g):
    t, d = h.shape
    nb = t // TM
    row_spec = pl.BlockSpec((TM, d), lambda i: (i, 0))
    return pl.pallas_call(
        _final_body,
        grid=(nb,),
        in_specs=[row_spec, row_spec, pl.BlockSpec((TM, d), lambda i: (i + nb, 0)),
                  pl.BlockSpec((1, d), lambda i: (0, 0))],
        out_specs=row_spec,
        out_shape=jax.ShapeDtypeStruct((t, d), F32),
        compiler_params=_cparams(("parallel",)),
        name="final_norm",
    )(h, y2, y2, g)


def _block_diag(w):
    nb, bd, _ = w.shape
    eye = jnp.eye(nb, dtype=w.dtype)
    return (eye[:, None, :, None] * w[:, :, None, :]).reshape(nb * bd, nb * bd)


def _router_weights(w_group, b_group, w_expert, b_expert):
    d = w_group.shape[0]
    wr = jnp.zeros((d, LANES), F32)
    wr = wr.at[:, 0:N_GROUPS].set(w_group).at[:, SUBLANES:SUBLANES + N_EXPERTS].set(w_expert)
    br = jnp.zeros((1, LANES), F32)
    br = br.at[0, 0:N_GROUPS].set(b_group).at[0, SUBLANES:SUBLANES + N_EXPERTS].set(b_expert)
    return wr, br


def _moe_layer(xn, ri, rw, w_up, w_down):
    t = xn.shape[0]
    tile_expert, num_tiles, tok, dst, gate = _dispatch(ri, rw, t)
    return _moe(xn, tile_expert, num_tiles, tok, dst, gate, w_up.astype(BF16), w_down.astype(BF16),
                TOP_K * t + TM_MOE)


def kernel(x, norm_mix_g, norm_ffn_g, ab_w_in, ab_w_out, lru_conv_w, lru_conv_b, lru_w_r, lru_b_r, lru_w_i,
           lru_b_i, lru_lambda, mlstm_b_i, mlstm_b_f, mlstm_norm_g, diff_w_qkv, diff_w_o, diff_lambda_q1,
           diff_lambda_k1, diff_lambda_q2, diff_lambda_k2, diff_norm_g, moe_w_group, moe_b_group, moe_w_expert,
           moe_b_expert, moe_w_up, moe_w_down, final_norm_g):
    batch, seq, d = x.shape
    t = batch * seq
    lru_width = lru_lambda.shape[1]
    m_heads, m_dh = mlstm_norm_g.shape[1], mlstm_norm_g.shape[2]
    m_width = m_heads * m_dh
    a_heads, a_dv = diff_norm_g.shape[1], diff_norm_g.shape[2]
    n_main = 2 * lru_width + 4 * m_width
    assert n_main + 2 * m_heads == ab_w_in.shape[2]
    assert seq % TS_LRU == 0 and seq % TQ == 0 and t % TM == 0 and (TOP_K * t) % TM_MOE == 0
    assert 2 * m_heads <= SUBLANES and TQ == TK and TQ % CHUNK == 0

    x2 = x.reshape(t, d)

    w_in = ab_w_in[0]
    w_gate = jnp.zeros((d, LANES), F32).at[:, 0:2 * m_heads].set(w_in[:, n_main:]).astype(BF16)
    b_gate = jnp.zeros((1, LANES), F32).at[0, 0:m_heads].set(mlstm_b_i[0]).at[0, m_heads:2 * m_heads].set(mlstm_b_f[0])
    pf, qkvo, gate = _inproj0(x2, norm_mix_g[0:1], w_in[:, 0:n_main].astype(BF16), w_gate, b_gate)

    w_ri = jnp.concatenate([_block_diag(lru_w_r[0]), _block_diag(lru_w_i[0])], axis=1).astype(BF16)
    b_ri = jnp.concatenate([lru_b_r[0], lru_b_i[0]])[None, :]
    ya = _lru(pf, lru_conv_w[0], lru_conv_b[0][None, :], w_ri, b_ri, lru_lambda[0][None, :], batch, seq)

    grow = gate[:, 0:SUBLANES].reshape(t // CHUNK, CHUNK, SUBLANES).transpose(0, 2, 1)
    yb = _mlstm(qkvo, gate, grow, mlstm_norm_g[0].reshape(1, m_width), batch, seq, m_heads, m_dh)

    w_out = ab_w_out[0].astype(BF16)
    wr, br = _router_weights(moe_w_group[0], moe_b_group[0], moe_w_expert[0], moe_b_expert[0])
    h1, xn, ri, rw = _outproj_router(x2, [ya, yb], [w_out[0:lru_width], w_out[lru_width:]], norm_ffn_g[0:1], wr, br)
    y2 = _moe_layer(xn, ri, rw, moe_w_up[0], moe_w_down[0])

    lambda_init = 0.8 - 0.6 * math.exp(-0.3 * 1)
    h2, qkv = _inproj1(h1, y2, norm_mix_g[1:2], diff_w_qkv[0].astype(BF16))
    attn = _attn(qkv, diff_lambda_q1[0][None, :], diff_lambda_k1[0][None, :], diff_lambda_q2[0][None, :],
                 diff_lambda_k2[0][None, :], diff_norm_g[0], batch, seq, a_heads, a_dv, lambda_init)
    wr, br = _router_weights(moe_w_group[1], moe_b_group[1], moe_w_expert[1], moe_b_expert[1])
    h3, xn, ri, rw = _outproj_router(h2, [attn], [diff_w_o[0].astype(BF16)], norm_ffn_g[1:2], wr, br)
    y2 = _moe_layer(xn, ri, rw, moe_w_up[1], moe_w_down[1])

    return _final(h3, y2, final_norm_g[None, :]).reshape(batch, seq, d)
```

```python
import functools
import math

import jax
import jax.numpy as jnp
from jax import lax
from jax.experimental import pallas as pl
from jax.experimental.pallas import tpu as pltpu

F32 = jnp.float32
BF16 = jnp.bfloat16
I32 = jnp.int32

RMS_EPS = 1e-6
CHUNK = 64
LRU_C = 8.0
N_GROUPS = 4
EXPERTS_PER_GROUP = 8
N_EXPERTS = N_GROUPS * EXPERTS_PER_GROUP
TOP_K = 2

LANES = 128
SUBLANES = 8
VMEM_LIMIT = 56 * 1024 * 1024

TM = 512
TS_LRU = 512
TQ = 256
TK = 256
TM_MOE = 512

NEG_INF = float("-inf")


def _cparams(sem):
    return pltpu.CompilerParams(dimension_semantics=sem, vmem_limit_bytes=VMEM_LIMIT)


def _dot(a, b, precision=None):
    return jnp.dot(a, b, preferred_element_type=F32, precision=precision)


def _dot_nt(a, b):
    return lax.dot_general(a, b, (((1,), (1,)), ((), ())), preferred_element_type=F32)


def _rms(x, g):
    ms = jnp.mean(x * x, axis=-1, keepdims=True)
    return x * lax.rsqrt(ms + RMS_EPS) * g


def _log_sigmoid(x):
    return jnp.minimum(x, 0.0) - jnp.log1p(jnp.exp(-jnp.abs(x)))


def _sigmoid(x):
    return 1.0 / (1.0 + jnp.exp(-x))


def _inproj0_body(x_ref, g_ref, w_ref, wg_ref, bg_ref, pf_ref, qkvo_ref, gate_ref):
    d = x_ref.shape[1]
    xb = _rms(x_ref[...], g_ref[...]).astype(BF16)
    pf_ref[...] = _dot(xb, w_ref[:, 0:d])
    for c in range(qkvo_ref.shape[1] // d):
        qkvo_ref[:, c * d:(c + 1) * d] = _dot(xb, w_ref[:, (c + 1) * d:(c + 2) * d]).astype(BF16)
    gate_ref[...] = _dot(xb, wg_ref[...]) + bg_ref[...]


def _inproj0(x2, g, w_main, w_gate, b_gate):
    t, d = x2.shape
    n_main = w_main.shape[1]
    grid = (t // TM,)
    return pl.pallas_call(
        _inproj0_body,
        grid=grid,
        in_specs=[
            pl.BlockSpec((TM, d), lambda i: (i, 0)),
            pl.BlockSpec((1, d), lambda i: (0, 0)),
            pl.BlockSpec((d, n_main), lambda i: (0, 0)),
            pl.BlockSpec((d, LANES), lambda i: (0, 0)),
            pl.BlockSpec((1, LANES), lambda i: (0, 0)),
        ],
        out_specs=[
            pl.BlockSpec((TM, d), lambda i: (i, 0)),
            pl.BlockSpec((TM, n_main - d), lambda i: (i, 0)),
            pl.BlockSpec((TM, LANES), lambda i: (i, 0)),
        ],
        out_shape=[
            jax.ShapeDtypeStruct((t, d), F32),
            jax.ShapeDtypeStruct((t, n_main - d), BF16),
            jax.ShapeDtypeStruct((t, LANES), F32),
        ],
        compiler_params=_cparams(("parallel",)),
        name="inproj0",
    )(x2, g, w_main, w_gate, b_gate)


def _lru_body(pf_ref, cw_ref, cb_ref, wri_ref, bri_ref, lam_ref, ya_ref, xbuf, a_s, u_s, hcar):
    ts = pf_ref.shape[0]
    c = ya_ref.shape[1]
    conv_w = cw_ref.shape[0]
    halo = SUBLANES

    @pl.when(pl.program_id(1) == 0)
    def _():
        xbuf[0:halo, :] = jnp.zeros((halo, c), F32)
        hcar[...] = jnp.zeros_like(hcar)

    xbuf[halo:, :] = pf_ref[:, 0:c]
    xc = cb_ref[...] + cw_ref[0:1, :] * xbuf[pl.ds(halo - conv_w + 1, ts), :]
    for j in range(1, conv_w):
        xc = xc + cw_ref[j:j + 1, :] * xbuf[pl.ds(halo - conv_w + 1 + j, ts), :]
    xbuf[0:halo, :] = xbuf[ts:ts + halo, :]

    ri = _dot(xc.astype(BF16), wri_ref[...]) + bri_ref[...]
    r = _sigmoid(ri[:, 0:c])
    i = _sigmoid(ri[:, c:2 * c])
    log_a = LRU_C * r * _log_sigmoid(lam_ref[...])
    a = jnp.exp(log_a)
    a_s[...] = a
    u_s[...] = jnp.sqrt(-jnp.tanh(log_a) * (a * a + 1.0)) * (i * xc)

    row = lax.broadcasted_iota(I32, (SUBLANES, c), 0)

    def blk(j, carry):
        r0 = pl.multiple_of(j * SUBLANES, SUBLANES)
        a = a_s[pl.ds(r0, SUBLANES), :]
        u = u_s[pl.ds(r0, SUBLANES), :]
        for sh in (1, 2, 4):
            a_sh = jnp.where(row >= sh, pltpu.roll(a, sh, 0), 1.0)
            u_sh = jnp.where(row >= sh, pltpu.roll(u, sh, 0), 0.0)
            u = a * u_sh + u
            a = a * a_sh
        h = a * carry + u
        u_s[pl.ds(r0, SUBLANES), :] = h
        return h[SUBLANES - 1:SUBLANES, :]

    hcar[...] = lax.fori_loop(0, ts // SUBLANES, blk, hcar[...], unroll=2)
    ya_ref[...] = (u_s[...] * jax.nn.gelu(pf_ref[:, c:2 * c])).astype(BF16)


def _lru(pf, conv_w, conv_b, w_ri, b_ri, lam, batch, seq):
    t = pf.shape[0]
    c = lam.shape[1]
    ns = seq // TS_LRU
    return pl.pallas_call(
        _lru_body,
        grid=(batch, ns),
        in_specs=[
            pl.BlockSpec((TS_LRU, 2 * c), lambda b, s: (b * ns + s, 0)),
            pl.BlockSpec(conv_w.shape, lambda b, s: (0, 0)),
            pl.BlockSpec((1, c), lambda b, s: (0, 0)),
            pl.BlockSpec((c, 2 * c), lambda b, s: (0, 0)),
            pl.BlockSpec((1, 2 * c), lambda b, s: (0, 0)),
            pl.BlockSpec((1, c), lambda b, s: (0, 0)),
        ],
        out_specs=pl.BlockSpec((TS_LRU, c), lambda b, s: (b * ns + s, 0)),
        out_shape=jax.ShapeDtypeStruct((t, c), BF16),
        scratch_shapes=[
            pltpu.VMEM((TS_LRU + SUBLANES, c), F32),
            pltpu.VMEM((TS_LRU, c), F32),
            pltpu.VMEM((TS_LRU, c), F32),
            pltpu.VMEM((1, c), F32),
        ],
        compiler_params=_cparams(("parallel", "arbitrary")),
        name="rglru",
    )(pf, conv_w, conv_b, w_ri, b_ri, lam)


def _mlstm_body(qkvo_ref, gate_ref, grow_ref, mg_ref, yb_ref, c_s, m_s, *, heads, dh):
    seq = qkvo_ref.shape[0]
    nc = seq // CHUNK
    width = heads * dh
    scale = dh ** -0.5
    hi = lax.Precision.HIGHEST

    c_s[...] = jnp.zeros_like(c_s)
    m_s[...] = jnp.zeros_like(m_s)

    rr = lax.broadcasted_iota(I32, (CHUNK, CHUNK), 0)
    cc = lax.broadcasted_iota(I32, (CHUNK, CHUNK), 1)
    causal = cc <= rr
    tril = causal.astype(F32)
    triu = (rr <= cc).astype(F32)
    ones_col = (lax.broadcasted_iota(I32, (CHUNK, dh), 1) == 0).astype(BF16)

    def step(ci, carry):
        r0 = pl.multiple_of(ci * CHUNK, CHUNK)
        gcol = gate_ref[pl.ds(r0, CHUNK), :]
        grow = grow_ref[ci]
        bcol = _dot(tril, _log_sigmoid(gcol), precision=hi)
        brow = _dot(_log_sigmoid(grow), triu, precision=hi)
        for h in range(heads):
            i_col = gcol[:, h:h + 1]
            b_col = bcol[:, heads + h:heads + h + 1]
            i_row = grow[h:h + 1, :]
            b_row = brow[heads + h:heads + h + 1, :]
            q = qkvo_ref[pl.ds(r0, CHUNK), h * dh:(h + 1) * dh]
            k = qkvo_ref[pl.ds(r0, CHUNK), width + h * dh:width + (h + 1) * dh]
            v = qkvo_ref[pl.ds(r0, CHUNK), 2 * width + h * dh:2 * width + (h + 1) * dh]
            o = qkvo_ref[pl.ds(r0, CHUNK), 3 * width + h * dh:3 * width + (h + 1) * dh]
            m_prev = m_s[h:h + 1, 0:1]
            cst = c_s[h]

            log_d = jnp.where(causal, b_col - b_row + i_row, NEG_INF)
            log_inter = b_col + m_prev
            m_t = jnp.maximum(log_inter, jnp.max(log_d, axis=-1, keepdims=True))
            w_intra = jnp.exp(log_d - m_t)
            w_inter = jnp.exp(log_inter - m_t)
            s = _dot_nt(q, k) * scale * w_intra
            v_aug = jnp.concatenate([v, ones_col], axis=1)
            res = w_inter * _dot(q, cst.astype(BF16)) + _dot(s.astype(BF16), v_aug)
            num = res[:, 0:dh]
            den = res[:, dh:dh + 1]
            hh = num / jnp.maximum(jnp.abs(den), jnp.exp(-m_t))
            y = _rms(hh, mg_ref[:, h * dh:(h + 1) * dh]) * _sigmoid(o.astype(F32))
            yb_ref[pl.ds(r0, CHUNK), h * dh:(h + 1) * dh] = y.astype(BF16)

            g = b_col[CHUNK - 1:CHUNK, :]
            log_w = g - b_col + i_col
            m_new = jnp.maximum(g + m_prev, jnp.max(log_w, axis=0, keepdims=True))
            w_state = jnp.exp(log_w - m_new) * scale
            decay = jnp.exp(g + m_prev - m_new)
            wv = (v_aug.astype(F32) * w_state).astype(BF16)
            upd = _dot(k.astype(F32).T.astype(BF16), wv)
            c_s[h] = decay * cst + upd
            m_s[h:h + 1, 0:1] = m_new
        return carry

    lax.fori_loop(0, nc, step, 0)


def _mlstm(qkvo, gate, grow, mnorm_g, batch, seq, heads, dh):
    t = qkvo.shape[0]
    width = heads * dh
    nc = seq // CHUNK
    return pl.pallas_call(
        functools.partial(_mlstm_body, heads=heads, dh=dh),
        grid=(batch,),
        in_specs=[
            pl.BlockSpec((seq, 4 * width), lambda b: (b, 0)),
            pl.BlockSpec((seq, LANES), lambda b: (b, 0)),
            pl.BlockSpec((nc, SUBLANES, CHUNK), lambda b: (b, 0, 0)),
            pl.BlockSpec((1, width), lambda b: (0, 0)),
        ],
        out_specs=pl.BlockSpec((seq, width), lambda b: (b, 0)),
        out_shape=jax.ShapeDtypeStruct((t, width), BF16),
        scratch_shapes=[
            pltpu.VMEM((heads, dh, 2 * dh), F32),
            pltpu.VMEM((SUBLANES, LANES), F32),
        ],
        compiler_params=_cparams(("parallel",)),
        name="mlstm",
    )(qkvo, gate, grow, mnorm_g)


def _route(logits_t):
    tm = logits_t.shape[1]
    row = lax.broadcasted_iota(I32, (SUBLANES, tm), 0)
    rowf = row.astype(F32)
    none = float(SUBLANES)
    gl = jnp.where(row < N_GROUPS, logits_t[0:SUBLANES], NEG_INF)
    gmax = jnp.max(gl, axis=0, keepdims=True)
    gidx = jnp.min(jnp.where(gl == gmax, rowf, none), axis=0, keepdims=True)
    gweight = 1.0 / jnp.sum(jnp.exp(gl - gmax), axis=0, keepdims=True)
    el = logits_t[SUBLANES:2 * SUBLANES]
    for g in range(1, N_GROUPS):
        el = jnp.where(gidx == float(g), logits_t[(g + 1) * SUBLANES:(g + 2) * SUBLANES], el)
    v1 = jnp.max(el, axis=0, keepdims=True)
    i1 = jnp.min(jnp.where(el == v1, rowf, none), axis=0, keepdims=True)
    el2 = jnp.where(rowf == i1, NEG_INF, el)
    v2 = jnp.max(el2, axis=0, keepdims=True)
    i2 = jnp.min(jnp.where(el2 == v2, rowf, none), axis=0, keepdims=True)
    t = jnp.exp(v2 - v1)
    w1 = gweight / (1.0 + t)
    w2 = gweight * t / (1.0 + t)
    e1 = (gidx * EXPERTS_PER_GROUP + i1).astype(I32)
    e2 = (gidx * EXPERTS_PER_GROUP + i2).astype(I32)
    ids = jnp.where(row == 0, e1, jnp.where(row == 1, e2, 0))
    wts = jnp.where(row == 0, w1, jnp.where(row == 1, w2, 0.0))
    return ids, wts


def _outproj_router_body(*refs, nact):
    h_ref = refs[0]
    act_refs = refs[1:1 + nact]
    w_refs = refs[1 + nact:1 + 2 * nact]
    g_ref, wr_ref, br_ref, hn_ref, xn_ref, ri_ref, rw_ref = refs[1 + 2 * nact:]
    acc = h_ref[...]
    for a_ref, w_ref in zip(act_refs, w_refs):
        acc = acc + _dot(a_ref[...], w_ref[...])
    hn_ref[...] = acc
    xn = _rms(acc, g_ref[...])
    xn_ref[...] = xn
    logits = _dot(xn, wr_ref[...], precision=lax.Precision.HIGHEST) + br_ref[...]
    ids, wts = _route(logits.T)
    ri_ref[...] = ids
    rw_ref[...] = wts


def _outproj_router(h, acts, ws, g, wr, br):
    t, d = h.shape
    nact = len(acts)
    row_spec = pl.BlockSpec((TM, d), lambda i: (i, 0))
    in_specs = [row_spec]
    in_specs += [pl.BlockSpec((TM, a.shape[1]), lambda i: (i, 0)) for a in acts]
    in_specs += [pl.BlockSpec(w.shape, lambda i: (0, 0)) for w in ws]
    in_specs += [
        pl.BlockSpec((1, d), lambda i: (0, 0)),
        pl.BlockSpec((d, LANES), lambda i: (0, 0)),
        pl.BlockSpec((1, LANES), lambda i: (0, 0)),
    ]
    return pl.pallas_call(
        functools.partial(_outproj_router_body, nact=nact),
        grid=(t // TM,),
        in_specs=in_specs,
        out_specs=[
            row_spec,
            row_spec,
            pl.BlockSpec((SUBLANES, TM), lambda i: (0, i)),
            pl.BlockSpec((SUBLANES, TM), lambda i: (0, i)),
        ],
        out_shape=[
            jax.ShapeDtypeStruct((t, d), F32),
            jax.ShapeDtypeStruct((t, d), F32),
            jax.ShapeDtypeStruct((SUBLANES, t), I32),
            jax.ShapeDtypeStruct((SUBLANES, t), F32),
        ],
        compiler_params=_cparams(("parallel",)),
        name="outproj_router",
    )(h, *acts, *ws, g, wr, br)


def _moe_body(te_ref, nt_ref, tok_ref, dst_ref, gate_ref, xn_hbm, wup_ref, wdn_ref, y_hbm,
              tok_s, dst_s, xbuf, ybuf, sem_i, sem_g, sem_s):
    i = pl.program_id(0)
    tm, d = xbuf.shape
    ff = wdn_ref.shape[0]

    @pl.when(i == 0)
    def _():
        ybuf[...] = jnp.zeros_like(ybuf)
        pad = pltpu.make_async_copy(ybuf, y_hbm.at[pl.ds(y_hbm.shape[0] - tm, tm), :], sem_s.at[0])
        pad.start()
        pad.wait()

    @pl.when(i < nt_ref[0])
    def _():
        ci = pltpu.make_async_copy(tok_ref, tok_s, sem_i.at[0])
        cd = pltpu.make_async_copy(dst_ref, dst_s, sem_i.at[1])
        ci.start()
        cd.start()
        ci.wait()
        cd.wait()

        def gather_copy(r):
            return pltpu.make_async_copy(xn_hbm.at[pl.ds(tok_s[0, 0, r], 1), :], xbuf.at[pl.ds(r, 1), :], sem_g.at[0])

        def scatter_copy(r):
            return pltpu.make_async_copy(ybuf.at[pl.ds(r, 1), :], y_hbm.at[pl.ds(dst_s[0, 0, r], 1), :], sem_s.at[0])

        def start_g(r, c):
            gather_copy(r).start()
            return c

        def wait_g(r, c):
            gather_copy(r).wait()
            return c

        lax.fori_loop(0, tm, start_g, 0, unroll=8)
        lax.fori_loop(0, tm, wait_g, 0, unroll=8)

        gate_t = jnp.broadcast_to(gate_ref[0], (LANES, tm)).T
        hcat = _dot(xbuf[...].astype(BF16), wup_ref[...])
        hh = jax.nn.silu(hcat[:, 0:ff]) * hcat[:, ff:2 * ff] * gate_t[:, 0:1]
        ybuf[...] = _dot(hh.astype(BF16), wdn_ref[...])

        def start_s(r, c):
            scatter_copy(r).start()
            return c

        def wait_s(r, c):
            scatter_copy(r).wait()
            return c

        lax.fori_loop(0, tm, start_s, 0, unroll=8)
        lax.fori_loop(0, tm, wait_s, 0, unroll=8)


def _moe(xn, tile_expert, num_tiles, tok, dst, gate, w_up, w_down, n_rows_out):
    t, d = xn.shape
    ntiles, _, tm = tok.shape
    ff = w_down.shape[1]
    tile_spec = pl.BlockSpec((1, 1, tm), lambda i, te, nt: (i, 0, 0))
    grid_spec = pltpu.PrefetchScalarGridSpec(
        num_scalar_prefetch=2,
        grid=(ntiles,),
        in_specs=[
            tile_spec,
            tile_spec,
            tile_spec,
            pl.BlockSpec(memory_space=pl.ANY),
            pl.BlockSpec((None, d, 2 * ff), lambda i, te, nt: (te[i], 0, 0)),
            pl.BlockSpec((None, ff, d), lambda i, te, nt: (te[i], 0, 0)),
        ],
        out_specs=pl.BlockSpec(memory_space=pl.ANY),
        scratch_shapes=[
            pltpu.SMEM((1, 1, tm), I32),
            pltpu.SMEM((1, 1, tm), I32),
            pltpu.VMEM((tm, d), F32),
            pltpu.VMEM((tm, d), F32),
            pltpu.SemaphoreType.DMA((2,)),
            pltpu.SemaphoreType.DMA((1,)),
            pltpu.SemaphoreType.DMA((1,)),
        ],
    )
    return pl.pallas_call(
        _moe_body,
        grid_spec=grid_spec,
        out_shape=jax.ShapeDtypeStruct((n_rows_out, d), F32),
        compiler_params=_cparams(("arbitrary",)),
        name="moe_experts",
    )(tile_expert, num_tiles, tok, dst, gate, xn, w_up, w_down)


def _dispatch(ri, rw, t):
    tm = TM_MOE
    n_assign = TOP_K * t
    ntiles = n_assign // tm + N_EXPERTS
    e_flat = ri[0:TOP_K].reshape(n_assign)
    w_flat = rw[0:TOP_K].reshape(n_assign)
    order = jnp.argsort(e_flat, stable=True).astype(I32)
    counts = jnp.sum((e_flat[:, None] == jnp.arange(N_EXPERTS, dtype=I32)[None, :]).astype(I32), axis=0)
    tiles_per = (counts + tm - 1) // tm
    tile_end = jnp.cumsum(tiles_per)
    num_tiles = tile_end[-1:].astype(I32)
    tile_ids = jnp.arange(ntiles, dtype=I32)
    tile_expert = jnp.minimum(jnp.sum((tile_ids[:, None] >= tile_end[None, :]).astype(I32), axis=1), N_EXPERTS - 1)
    last_valid = jnp.take(tile_expert, jnp.maximum(num_tiles[0] - 1, 0))
    tile_expert = jnp.where(tile_ids < num_tiles[0], tile_expert, last_valid).astype(I32)
    start = jnp.cumsum(counts) - counts
    tile_start = (tile_end - tiles_per) * tm
    pos = jnp.arange(ntiles * tm, dtype=I32)
    e_pos = jnp.repeat(tile_expert, tm)
    off = pos - jnp.take(tile_start, e_pos)
    valid = (off < jnp.take(counts, e_pos)) & (pos < num_tiles[0] * tm)
    a = jnp.take(order, jnp.clip(jnp.take(start, e_pos) + off, 0, n_assign - 1))
    tok = jnp.where(valid, a % t, 0).astype(I32).reshape(ntiles, 1, tm)
    dst = jnp.where(valid, a, n_assign + pos % tm).astype(I32).reshape(ntiles, 1, tm)
    gate = jnp.where(valid, jnp.take(w_flat, a), 0.0).reshape(ntiles, 1, tm)
    return tile_expert, num_tiles, tok, dst, gate


def _inproj1_body(h_ref, y0_ref, y1_ref, g_ref, w_ref, hn_ref, qkv_ref):
    d = h_ref.shape[1]
    h = h_ref[...] + y0_ref[...] + y1_ref[...]
    hn_ref[...] = h
    xb = _rms(h, g_ref[...]).astype(BF16)
    for c in range(w_ref.shape[1] // d):
        qkv_ref[:, c * d:(c + 1) * d] = _dot(xb, w_ref[:, c * d:(c + 1) * d]).astype(BF16)


def _inproj1(h, y2, g, w):
    t, d = h.shape
    n = w.shape[1]
    nb = t // TM
    row_spec = pl.BlockSpec((TM, d), lambda i: (i, 0))
    return pl.pallas_call(
        _inproj1_body,
        grid=(nb,),
        in_specs=[
            row_spec,
            row_spec,
            pl.BlockSpec((TM, d), lambda i: (i + nb, 0)),
            pl.BlockSpec((1, d), lambda i: (0, 0)),
            pl.BlockSpec((d, n), lambda i: (0, 0)),
        ],
        out_specs=[row_spec, pl.BlockSpec((TM, n), lambda i: (i, 0))],
        out_shape=[jax.ShapeDtypeStruct((t, d), F32), jax.ShapeDtypeStruct((t, n), BF16)],
        compiler_params=_cparams(("parallel",)),
        name="inproj1",
    )(h, y2, y2, g, w)


def _attn_body(q_ref, k_ref, v_ref, lq1_ref, lk1_ref, lq2_ref, lk2_ref, ng_ref, o_ref,
               vaug_s, bias_s, s_s, p_s, m_s, *, heads, lambda_init):
    h = pl.program_id(1)
    seq, dv = q_ref.shape
    nq = seq // TQ
    dqk = dv // 2
    slope = jnp.exp2(jnp.full((1, 1), -8.0 / heads, F32) * (h + 1).astype(F32))
    scale = dqk ** -0.5

    vaug_s[:, 0:dv] = v_ref[...]
    vaug_s[:, dv:2 * dv] = (lax.broadcasted_iota(I32, (seq, dv), 1) == 0).astype(BF16)

    rowp = lax.broadcasted_iota(I32, (2 * TQ, TK), 0) % TQ
    colp = lax.broadcasted_iota(I32, (2 * TQ, TK), 1)
    allowed = (colp // CHUNK) <= (rowp // CHUNK)
    rowf = rowp.astype(F32)
    bias_s[...] = jnp.where(allowed, slope * (rowf - jnp.abs(rowf - colp.astype(F32))), NEG_INF)
    kcol = slope * lax.broadcasted_iota(I32, (1, TK), 1).astype(F32)
    lam = (jnp.exp(jnp.sum(lq1_ref[...] * lk1_ref[...], axis=-1, keepdims=True))
           - jnp.exp(jnp.sum(lq2_ref[...] * lk2_ref[...], axis=-1, keepdims=True)) + lambda_init)
    gain = ng_ref[pl.ds(h, 1), :] * (1.0 - lambda_init)
    lane = lax.broadcasted_iota(I32, (TQ, dv), 1)

    for qi in range(nq):
        q = q_ref[qi * TQ:(qi + 1) * TQ, :].astype(F32) * scale
        qs = jnp.concatenate([jnp.where(lane < dqk, q, 0.0), jnp.where(lane >= dqk, q, 0.0)], axis=0).astype(BF16)
        mlane = None
        for j in range(qi + 1):
            s = _dot_nt(qs, k_ref[j * TK:(j + 1) * TK, :])
            if j < qi:
                s = s + (kcol - slope * float((qi - j) * TK))
            else:
                s = s + bias_s[...]
            s_s[:, j * TK:(j + 1) * TK] = s
            for c in range(TK // LANES):
                part = s[:, c * LANES:(c + 1) * LANES]
                mlane = part if mlane is None else jnp.maximum(mlane, part)
        m_s[...] = jnp.broadcast_to(jnp.max(mlane, axis=-1, keepdims=True), m_s.shape)
        for j in range(qi + 1):
            for c in range(TK // LANES):
                lo = j * TK + c * LANES
                p_s[:, lo:lo + LANES] = jnp.exp(s_s[:, lo:lo + LANES] - m_s[...]).astype(BF16)
        kv = (qi + 1) * TK
        acc = _dot(p_s[:, 0:kv], vaug_s[0:kv, :])
        o1 = acc[0:TQ, 0:dv] / acc[0:TQ, dv:dv + 1]
        o2 = acc[TQ:2 * TQ, 0:dv] / acc[TQ:2 * TQ, dv:dv + 1]
        o_ref[qi * TQ:(qi + 1) * TQ, :] = _rms(o1 - lam * o2, gain).astype(BF16)


def _attn(qkv, lq1, lk1, lq2, lk2, norm_g, batch, seq, heads, dv, lambda_init):
    t = qkv.shape[0]
    dqk = dv // 2
    small = pl.BlockSpec((1, dqk), lambda b, h: (0, 0))
    return pl.pallas_call(
        functools.partial(_attn_body, heads=heads, lambda_init=lambda_init),
        grid=(batch, heads),
        in_specs=[
            pl.BlockSpec((seq, dv), lambda b, h: (b, h)),
            pl.BlockSpec((seq, dv), lambda b, h: (b, heads + h)),
            pl.BlockSpec((seq, dv), lambda b, h: (b, 2 * heads + h)),
            small, small, small, small,
            pl.BlockSpec((heads, dv), lambda b, h: (0, 0)),
        ],
        out_specs=pl.BlockSpec((seq, dv), lambda b, h: (b, h)),
        out_shape=jax.ShapeDtypeStruct((t, heads * dv), BF16),
        scratch_shapes=[
            pltpu.VMEM((seq, 2 * dv), BF16),
            pltpu.VMEM((2 * TQ, TK), F32),
            pltpu.VMEM((2 * TQ, seq), F32),
            pltpu.VMEM((2 * TQ, seq), BF16),
            pltpu.VMEM((2 * TQ, LANES), F32),
        ],
        compiler_params=_cparams(("parallel", "parallel")),
        name="diff_attn",
    )(qkv, qkv, qkv, lq1, lk1, lq2, lk2, norm_g)


def _final_body(h_ref, y0_ref, y1_ref, g_ref, o_ref):
    o_ref[...] = _rms(h_ref[...] + y0_ref[...] + y1_ref[...], g_ref[...])


def _final(h, y2, g):
    t, d = h.shape
    nb = t // TM
    row_spec = pl.BlockSpec((TM, d), lambda i: (i, 0))
    return pl.pallas_call(
        _final_body,
        grid=(nb,),
        in_specs=[row_spec, row_spec, pl.BlockSpec((TM, d), lambda i: (i + nb, 0)),
                  pl.BlockSpec((1, d), lambda i: (0, 0))],
        out_specs=row_spec,
        out_shape=jax.ShapeDtypeStruct((t, d), F32),
        compiler_params=_cparams(("parallel",)),
        name="final_norm",
    )(h, y2, y2, g)


def _block_diag(w):
    nb, bd, _ = w.shape
    eye = jnp.eye(nb, dtype=w.dtype)
    return (eye[:, None, :, None] * w[:, :, None, :]).reshape(nb * bd, nb * bd)


def _router_weights(w_group, b_group, w_expert, b_expert):
    d = w_group.shape[0]
    wr = jnp.zeros((d, LANES), F32)
    wr = wr.at[:, 0:N_GROUPS].set(w_group).at[:, SUBLANES:SUBLANES + N_EXPERTS].set(w_expert)
    br = jnp.zeros((1, LANES), F32)
    br = br.at[0, 0:N_GROUPS].set(b_group).at[0, SUBLANES:SUBLANES + N_EXPERTS].set(b_expert)
    return wr, br


def _moe_layer(xn, ri, rw, w_up, w_down):
    t = xn.shape[0]
    tile_expert, num_tiles, tok, dst, gate = _dispatch(ri, rw, t)
    return _moe(xn, tile_expert, num_tiles, tok, dst, gate, w_up.astype(BF16), w_down.astype(BF16),
                TOP_K * t + TM_MOE)


def kernel(x, norm_mix_g, norm_ffn_g, ab_w_in, ab_w_out, lru_conv_w, lru_conv_b, lru_w_r, lru_b_r, lru_w_i,
           lru_b_i, lru_lambda, mlstm_b_i, mlstm_b_f, mlstm_norm_g, diff_w_qkv, diff_w_o, diff_lambda_q1,
           diff_lambda_k1, diff_lambda_q2, diff_lambda_k2, diff_norm_g, moe_w_group, moe_b_group, moe_w_expert,
           moe_b_expert, moe_w_up, moe_w_down, final_norm_g):
    batch, seq, d = x.shape
    t = batch * seq
    lru_width = lru_lambda.shape[1]
    m_heads, m_dh = mlstm_norm_g.shape[1], mlstm_norm_g.shape[2]
    m_width = m_heads * m_dh
    a_heads, a_dv = diff_norm_g.shape[1], diff_norm_g.shape[2]
    n_main = 2 * lru_width + 4 * m_width
    assert n_main + 2 * m_heads == ab_w_in.shape[2]
    assert seq % TS_LRU == 0 and seq % TQ == 0 and t % TM == 0 and (TOP_K * t) % TM_MOE == 0
    assert 2 * m_heads <= SUBLANES and TQ == TK and TQ % CHUNK == 0

    x2 = x.reshape(t, d)

    w_in = ab_w_in[0]
    w_gate = jnp.zeros((d, LANES), F32).at[:, 0:2 * m_heads].set(w_in[:, n_main:]).astype(BF16)
    b_gate = jnp.zeros((1, LANES), F32).at[0, 0:m_heads].set(mlstm_b_i[0]).at[0, m_heads:2 * m_heads].set(mlstm_b_f[0])
    pf, qkvo, gate = _inproj0(x2, norm_mix_g[0:1], w_in[:, 0:n_main].astype(BF16), w_gate, b_gate)

    w_ri = jnp.concatenate([_block_diag(lru_w_r[0]), _block_diag(lru_w_i[0])], axis=1).astype(BF16)
    b_ri = jnp.concatenate([lru_b_r[0], lru_b_i[0]])[None, :]
    ya = _lru(pf, lru_conv_w[0], lru_conv_b[0][None, :], w_ri, b_ri, lru_lambda[0][None, :], batch, seq)

    grow = gate[:, 0:SUBLANES].reshape(t // CHUNK, CHUNK, SUBLANES).transpose(0, 2, 1)
    yb = _mlstm(qkvo, gate, grow, mlstm_norm_g[0].reshape(1, m_width), batch, seq, m_heads, m_dh)

    w_out = ab_w_out[0].astype(BF16)
    wr, br = _router_weights(moe_w_group[0], moe_b_group[0], moe_w_expert[0], moe_b_expert[0])
    h1, xn, ri, rw = _outproj_router(x2, [ya, yb], [w_out[0:lru_width], w_out[lru_width:]], norm_ffn_g[0:1], wr, br)
    y2 = _moe_layer(xn, ri, rw, moe_w_up[0], moe_w_down[0])

    lambda_init = 0.8 - 0.6 * math.exp(-0.3 * 1)
    h2, qkv = _inproj1(h1, y2, norm_mix_g[1:2], diff_w_qkv[0].astype(BF16))
    attn = _attn(qkv, diff_lambda_q1[0][None, :], diff_lambda_k1[0][None, :], diff_lambda_q2[0][None, :],
                 diff_lambda_k2[0][None, :], diff_norm_g[0], batch, seq, a_heads, a_dv, lambda_init)
    wr, br = _router_weights(moe_w_group[1], moe_b_group[1], moe_w_expert[1], moe_b_expert[1])
    h3, xn, ri, rw = _outproj_router(h2, [attn], [diff_w_o[0].astype(BF16)], norm_ffn_g[1:2], wr, br)
    y2 = _moe_layer(xn, ri, rw, moe_w_up[1], moe_w_down[1])

    return _final(h3, y2, final_norm_g[None, :]).reshape(batch, seq, d)
```

```python
import functools
import math

import jax
import jax.numpy as jnp
from jax import lax
from jax.experimental import pallas as pl
from jax.experimental.pallas import tpu as pltpu

F32 = jnp.float32
BF16 = jnp.bfloat16
I32 = jnp.int32

RMS_EPS = 1e-6
CHUNK = 64
LRU_C = 8.0
N_GROUPS = 4
EXPERTS_PER_GROUP = 8
N_EXPERTS = N_GROUPS * EXPERTS_PER_GROUP
TOP_K = 2

LANES = 128
SUBLANES = 8
VMEM_LIMIT = 56 * 1024 * 1024

TM = 512
TS_LRU = 512
TQ = 256
TK = 256
TM_MOE = 512
MLSTM_SEQS = 2
MOE_COL_CHUNKS = 4

NEG_INF = float("-inf")


def _cparams(sem):
    return pltpu.CompilerParams(dimension_semantics=sem, vmem_limit_bytes=VMEM_LIMIT)


def _dot(a, b, precision=None):
    return jnp.dot(a, b, preferred_element_type=F32, precision=precision)


def _dot_nt(a, b):
    return lax.dot_general(a, b, (((1,), (1,)), ((), ())), preferred_element_type=F32)


def _rms(x, g):
    ms = jnp.mean(x * x, axis=-1, keepdims=True)
    return x * lax.rsqrt(ms + RMS_EPS) * g


def _log_sigmoid(x):
    return jnp.minimum(x, 0.0) - jnp.log1p(jnp.exp(-jnp.abs(x)))


def _sigmoid(x):
    return 1.0 / (1.0 + jnp.exp(-x))


def _slab_rows(d):
    return d // LANES


def _store_slabs(ref, x, lead=()):
    m, d = x.shape
    sr = _slab_rows(d)
    for c in range(sr):
        ref[lead + (pl.ds(c, m, stride=sr), slice(None))] = x[:, c * LANES:(c + 1) * LANES]


def _load_slabs(ref, m, d, lead=()):
    sr = _slab_rows(d)
    return [ref[lead + (pl.ds(c, m, stride=sr), slice(None))] for c in range(sr)]


def _inproj0_body(x_ref, g_ref, w_ref, wg_ref, bg_ref, pf_ref, qkvo_ref, gate_ref):
    d = x_ref.shape[1]
    xb = _rms(x_ref[...], g_ref[...]).astype(BF16)
    pf_ref[...] = _dot(xb, w_ref[:, 0:d])
    for c in range(qkvo_ref.shape[1] // d):
        qkvo_ref[:, c * d:(c + 1) * d] = _dot(xb, w_ref[:, (c + 1) * d:(c + 2) * d]).astype(BF16)
    gate_ref[...] = _dot(xb, wg_ref[...]) + bg_ref[...]


def _inproj0(x2, g, w_main, w_gate, b_gate):
    t, d = x2.shape
    n_main = w_main.shape[1]
    grid = (t // TM,)
    return pl.pallas_call(
        _inproj0_body,
        grid=grid,
        in_specs=[
            pl.BlockSpec((TM, d), lambda i: (i, 0)),
            pl.BlockSpec((1, d), lambda i: (0, 0)),
            pl.BlockSpec((d, n_main), lambda i: (0, 0)),
            pl.BlockSpec((d, LANES), lambda i: (0, 0)),
            pl.BlockSpec((1, LANES), lambda i: (0, 0)),
        ],
        out_specs=[
            pl.BlockSpec((TM, d), lambda i: (i, 0)),
            pl.BlockSpec((TM, n_main - d), lambda i: (i, 0)),
            pl.BlockSpec((TM, LANES), lambda i: (i, 0)),
        ],
        out_shape=[
            jax.ShapeDtypeStruct((t, d), F32),
            jax.ShapeDtypeStruct((t, n_main - d), BF16),
            jax.ShapeDtypeStruct((t, LANES), F32),
        ],
        compiler_params=_cparams(("parallel",)),
        name="inproj0",
    )(x2, g, w_main, w_gate, b_gate)


def _lru_body(pf_ref, cw_ref, cb_ref, wri_ref, bri_ref, lam_ref, ya_ref, xbuf, a_s, u_s, hcar):
    ts = pf_ref.shape[0]
    c = ya_ref.shape[1]
    conv_w = cw_ref.shape[0]
    halo = SUBLANES

    @pl.when(pl.program_id(1) == 0)
    def _():
        xbuf[0:halo, :] = jnp.zeros((halo, c), F32)
        hcar[...] = jnp.zeros_like(hcar)

    xbuf[halo:, :] = pf_ref[:, 0:c]
    xc = cb_ref[...] + cw_ref[0:1, :] * xbuf[pl.ds(halo - conv_w + 1, ts), :]
    for j in range(1, conv_w):
        xc = xc + cw_ref[j:j + 1, :] * xbuf[pl.ds(halo - conv_w + 1 + j, ts), :]
    xbuf[0:halo, :] = xbuf[ts:ts + halo, :]

    ri = _dot(xc.astype(BF16), wri_ref[...]) + bri_ref[...]
    r = _sigmoid(ri[:, 0:c])
    i = _sigmoid(ri[:, c:2 * c])
    log_a = LRU_C * r * _log_sigmoid(lam_ref[...])
    a = jnp.exp(log_a)
    a_s[...] = a
    u_s[...] = jnp.sqrt(-jnp.tanh(log_a) * (a * a + 1.0)) * (i * xc)

    row = lax.broadcasted_iota(I32, (SUBLANES, c), 0)

    def blk(j, carry):
        r0 = pl.multiple_of(j * SUBLANES, SUBLANES)
        a = a_s[pl.ds(r0, SUBLANES), :]
        u = u_s[pl.ds(r0, SUBLANES), :]
        for sh in (1, 2, 4):
            a_sh = jnp.where(row >= sh, pltpu.roll(a, sh, 0), 1.0)
            u_sh = jnp.where(row >= sh, pltpu.roll(u, sh, 0), 0.0)
            u = a * u_sh + u
            a = a * a_sh
        h = a * carry + u
        u_s[pl.ds(r0, SUBLANES), :] = h
        return h[SUBLANES - 1:SUBLANES, :]

    hcar[...] = lax.fori_loop(0, ts // SUBLANES, blk, hcar[...], unroll=2)
    ya_ref[...] = (u_s[...] * jax.nn.gelu(pf_ref[:, c:2 * c])).astype(BF16)


def _lru(pf, conv_w, conv_b, w_ri, b_ri, lam, batch, seq):
    t = pf.shape[0]
    c = lam.shape[1]
    ns = seq // TS_LRU
    return pl.pallas_call(
        _lru_body,
        grid=(batch, ns),
        in_specs=[
            pl.BlockSpec((TS_LRU, 2 * c), lambda b, s: (b * ns + s, 0)),
            pl.BlockSpec(conv_w.shape, lambda b, s: (0, 0)),
            pl.BlockSpec((1, c), lambda b, s: (0, 0)),
            pl.BlockSpec((c, 2 * c), lambda b, s: (0, 0)),
            pl.BlockSpec((1, 2 * c), lambda b, s: (0, 0)),
            pl.BlockSpec((1, c), lambda b, s: (0, 0)),
        ],
        out_specs=pl.BlockSpec((TS_LRU, c), lambda b, s: (b * ns + s, 0)),
        out_shape=jax.ShapeDtypeStruct((t, c), BF16),
        scratch_shapes=[
            pltpu.VMEM((TS_LRU + SUBLANES, c), F32),
            pltpu.VMEM((TS_LRU, c), F32),
            pltpu.VMEM((TS_LRU, c), F32),
            pltpu.VMEM((1, c), F32),
        ],
        compiler_params=_cparams(("parallel", "arbitrary")),
        name="rglru",
    )(pf, conv_w, conv_b, w_ri, b_ri, lam)


def _mlstm_body(qkvo_ref, gate_ref, grow_ref, mg_ref, yb_ref, c_s, m_s, *, heads, dh, nseq):
    seq = qkvo_ref.shape[0] // nseq
    nc = seq // CHUNK
    width = heads * dh
    scale = dh ** -0.5
    hi = lax.Precision.HIGHEST

    c_s[...] = jnp.zeros_like(c_s)
    m_s[...] = jnp.zeros_like(m_s)

    rr = lax.broadcasted_iota(I32, (CHUNK, CHUNK), 0)
    cc = lax.broadcasted_iota(I32, (CHUNK, CHUNK), 1)
    causal = cc <= rr
    tril = causal.astype(F32)
    triu = (rr <= cc).astype(F32)
    ones_col = (lax.broadcasted_iota(I32, (CHUNK, dh), 1) == 0).astype(BF16)

    def step(ci, carry):
        for sq in range(nseq):
            chunk_step(ci, sq)
        return carry

    def chunk_step(ci, sq):
        r0 = pl.multiple_of(sq * seq + ci * CHUNK, CHUNK)
        gcol = gate_ref[pl.ds(r0, CHUNK), :]
        grow = grow_ref[sq * nc + ci]
        bcol = _dot(tril, _log_sigmoid(gcol), precision=hi)
        brow = _dot(_log_sigmoid(grow), triu, precision=hi)
        for h in range(heads):
            st = sq * heads + h
            i_col = gcol[:, h:h + 1]
            b_col = bcol[:, heads + h:heads + h + 1]
            i_row = grow[h:h + 1, :]
            b_row = brow[heads + h:heads + h + 1, :]
            q = qkvo_ref[pl.ds(r0, CHUNK), h * dh:(h + 1) * dh]
            k = qkvo_ref[pl.ds(r0, CHUNK), width + h * dh:width + (h + 1) * dh]
            v = qkvo_ref[pl.ds(r0, CHUNK), 2 * width + h * dh:2 * width + (h + 1) * dh]
            o = qkvo_ref[pl.ds(r0, CHUNK), 3 * width + h * dh:3 * width + (h + 1) * dh]
            m_prev = m_s[st:st + 1, 0:1]
            cst = c_s[st]

            log_d = jnp.where(causal, b_col - b_row + i_row, NEG_INF)
            log_inter = b_col + m_prev
            m_t = jnp.maximum(log_inter, jnp.max(log_d, axis=-1, keepdims=True))
            w_intra = jnp.exp(log_d - m_t)
            w_inter = jnp.exp(log_inter - m_t)
            s = _dot_nt(q, k) * scale * w_intra
            v_aug = jnp.concatenate([v, ones_col], axis=1)
            res = w_inter * _dot(q, cst.astype(BF16)) + _dot(s.astype(BF16), v_aug)
            num = res[:, 0:dh]
            den = res[:, dh:dh + 1]
            hh = num / jnp.maximum(jnp.abs(den), jnp.exp(-m_t))
            y = _rms(hh, mg_ref[:, h * dh:(h + 1) * dh]) * _sigmoid(o.astype(F32))
            yb_ref[pl.ds(r0, CHUNK), h * dh:(h + 1) * dh] = y.astype(BF16)

            g = b_col[CHUNK - 1:CHUNK, :]
            log_w = g - b_col + i_col
            m_new = jnp.maximum(g + m_prev, jnp.max(log_w, axis=0, keepdims=True))
            w_state = jnp.exp(log_w - m_new) * scale
            decay = jnp.exp(g + m_prev - m_new)
            wv = (v_aug.astype(F32) * w_state).astype(BF16)
            upd = _dot(k.astype(F32).T.astype(BF16), wv)
            c_s[st] = decay * cst + upd
            m_s[st:st + 1, 0:1] = m_new

    lax.fori_loop(0, nc, step, 0)


def _mlstm(qkvo, gate, grow, mnorm_g, batch, seq, heads, dh):
    t = qkvo.shape[0]
    width = heads * dh
    nc = seq // CHUNK
    nseq = MLSTM_SEQS if batch % MLSTM_SEQS == 0 else 1
    assert nseq * heads <= SUBLANES
    rows = nseq * seq
    return pl.pallas_call(
        functools.partial(_mlstm_body, heads=heads, dh=dh, nseq=nseq),
        grid=(batch // nseq,),
        in_specs=[
            pl.BlockSpec((rows, 4 * width), lambda b: (b, 0)),
            pl.BlockSpec((rows, LANES), lambda b: (b, 0)),
            pl.BlockSpec((nseq * nc, SUBLANES, CHUNK), lambda b: (b, 0, 0)),
            pl.BlockSpec((1, width), lambda b: (0, 0)),
        ],
        out_specs=pl.BlockSpec((rows, width), lambda b: (b, 0)),
        out_shape=jax.ShapeDtypeStruct((t, width), BF16),
        scratch_shapes=[
            pltpu.VMEM((nseq * heads, dh, 2 * dh), F32),
            pltpu.VMEM((SUBLANES, LANES), F32),
        ],
        compiler_params=_cparams(("parallel",)),
        name="mlstm",
    )(qkvo, gate, grow, mnorm_g)


def _route(logits_t):
    tm = logits_t.shape[1]
    row = lax.broadcasted_iota(I32, (SUBLANES, tm), 0)
    rowf = row.astype(F32)
    none = float(SUBLANES)
    gl = jnp.where(row < N_GROUPS, logits_t[0:SUBLANES], NEG_INF)
    gmax = jnp.max(gl, axis=0, keepdims=True)
    gidx = jnp.min(jnp.where(gl == gmax, rowf, none), axis=0, keepdims=True)
    gweight = 1.0 / jnp.sum(jnp.exp(gl - gmax), axis=0, keepdims=True)
    el = logits_t[SUBLANES:2 * SUBLANES]
    for g in range(1, N_GROUPS):
        el = jnp.where(gidx == float(g), logits_t[(g + 1) * SUBLANES:(g + 2) * SUBLANES], el)
    v1 = jnp.max(el, axis=0, keepdims=True)
    i1 = jnp.min(jnp.where(el == v1, rowf, none), axis=0, keepdims=True)
    el2 = jnp.where(rowf == i1, NEG_INF, el)
    v2 = jnp.max(el2, axis=0, keepdims=True)
    i2 = jnp.min(jnp.where(el2 == v2, rowf, none), axis=0, keepdims=True)
    t = jnp.exp(v2 - v1)
    w1 = gweight / (1.0 + t)
    w2 = gweight * t / (1.0 + t)
    e1 = (gidx * EXPERTS_PER_GROUP + i1).astype(I32)
    e2 = (gidx * EXPERTS_PER_GROUP + i2).astype(I32)
    ids = jnp.where(row == 0, e1, jnp.where(row == 1, e2, 0))
    wts = jnp.where(row == 0, w1, jnp.where(row == 1, w2, 0.0))
    return ids, wts


def _outproj_router_body(*refs, nact):
    h_ref = refs[0]
    act_refs = refs[1:1 + nact]
    w_refs = refs[1 + nact:1 + 2 * nact]
    g_ref, wr_ref, br_ref, hn_ref, xn_ref, ri_ref, rw_ref = refs[1 + 2 * nact:]
    acc = h_ref[...]
    for a_ref, w_ref in zip(act_refs, w_refs):
        acc = acc + _dot(a_ref[...], w_ref[...])
    hn_ref[...] = acc
    xn = _rms(acc, g_ref[...])
    _store_slabs(xn_ref, xn)
    logits = _dot(xn, wr_ref[...], precision=lax.Precision.HIGHEST) + br_ref[...]
    ids, wts = _route(logits.T)
    ri_ref[...] = ids
    rw_ref[...] = wts


def _outproj_router(h, acts, ws, g, wr, br):
    t, d = h.shape
    nact = len(acts)
    row_spec = pl.BlockSpec((TM, d), lambda i: (i, 0))
    in_specs = [row_spec]
    in_specs += [pl.BlockSpec((TM, a.shape[1]), lambda i: (i, 0)) for a in acts]
    in_specs += [pl.BlockSpec(w.shape, lambda i: (0, 0)) for w in ws]
    in_specs += [
        pl.BlockSpec((1, d), lambda i: (0, 0)),
        pl.BlockSpec((d, LANES), lambda i: (0, 0)),
        pl.BlockSpec((1, LANES), lambda i: (0, 0)),
    ]
    return pl.pallas_call(
        functools.partial(_outproj_router_body, nact=nact),
        grid=(t // TM,),
        in_specs=in_specs,
        out_specs=[
            row_spec,
            pl.BlockSpec((TM * _slab_rows(d), LANES), lambda i: (i, 0)),
            pl.BlockSpec((SUBLANES, TM), lambda i: (0, i)),
            pl.BlockSpec((SUBLANES, TM), lambda i: (0, i)),
        ],
        out_shape=[
            jax.ShapeDtypeStruct((t, d), F32),
            jax.ShapeDtypeStruct((t * _slab_rows(d), LANES), F32),
            jax.ShapeDtypeStruct((SUBLANES, t), I32),
            jax.ShapeDtypeStruct((SUBLANES, t), F32),
        ],
        compiler_params=_cparams(("parallel",)),
        name="outproj_router",
    )(h, *acts, *ws, g, wr, br)


def _moe_body(te_ref, nt_ref, tok0_ref, tok_ref, dst_ref, xn_hbm, wup_ref, wdn_ref, y_hbm,
              tok_s, dst_s, xbuf, ybuf, sem_i, sem_g, sem_s):
    i = pl.program_id(0)
    nt = nt_ref[0]
    tm = tok_ref.shape[2]
    d, ff2 = wup_ref.shape
    ff = ff2 // 2
    sr = _slab_rows(d)
    slot = i % 2
    other = 1 - slot
    n_pieces = 2 * MOE_COL_CHUNKS
    rows_per_piece = tm // n_pieces

    def gather(r, s):
        return pltpu.make_async_copy(xn_hbm.at[pl.ds(tok_s[0, 0, r], sr), :], xbuf.at[s, pl.ds(r * sr, sr), :],
                                     sem_g.at[0])

    def scatter(r, s):
        return pltpu.make_async_copy(ybuf.at[s, pl.ds(r * sr, sr), :], y_hbm.at[pl.ds(dst_s[0, 0, r], sr), :],
                                     sem_s.at[0])

    def wait_all(copy, s):
        def body(r, c):
            copy(r, s).wait()
            return c
        lax.fori_loop(0, tm, body, 0, unroll=8)

    @pl.when(i == 0)
    def _():
        ybuf[...] = jnp.zeros_like(ybuf)
        pad = pltpu.make_async_copy(ybuf.at[0], y_hbm.at[pl.ds(y_hbm.shape[0] - tm * sr, tm * sr), :], sem_s.at[0])
        pad.start()
        pad.wait()
        c0 = pltpu.make_async_copy(tok0_ref, tok_s, sem_i.at[0])
        c0.start()
        c0.wait()

        def body(r, c):
            gather(r, 0).start()
            return c
        lax.fori_loop(0, tm, body, 0, unroll=8)

    @pl.when(i <= nt + 1)
    def _():
        wait_all(gather, slot)

        @pl.when(i >= 1)
        def _():
            wait_all(scatter, slot)

        ci = pltpu.make_async_copy(tok_ref, tok_s, sem_i.at[0])
        cd = pltpu.make_async_copy(dst_ref, dst_s, sem_i.at[1])
        ci.start()
        cd.start()
        ci.wait()
        cd.wait()

        def issue(piece):
            for r in range(piece * rows_per_piece, (piece + 1) * rows_per_piece):
                gather(r, other).start()
                scatter(r, other).start()

        x = jnp.concatenate([p.astype(BF16) for p in _load_slabs(xbuf, tm, d, lead=(slot,))], axis=1)
        cw = ff2 // MOE_COL_CHUNKS
        parts = []
        for c in range(MOE_COL_CHUNKS):
            issue(c)
            parts.append(_dot(x, wup_ref[:, c * cw:(c + 1) * cw].astype(BF16)))
        hcat = jnp.concatenate(parts, axis=1)
        hh = (jax.nn.silu(hcat[:, 0:ff]) * hcat[:, ff:ff2]).astype(BF16)
        cw = d // MOE_COL_CHUNKS
        parts = []
        for c in range(MOE_COL_CHUNKS):
            issue(MOE_COL_CHUNKS + c)
            parts.append(_dot(hh, wdn_ref[:, c * cw:(c + 1) * cw].astype(BF16)))
        _store_slabs(ybuf, jnp.concatenate(parts, axis=1), lead=(slot,))

        @pl.when(i == nt + 1)
        def _():
            wait_all(scatter, other)
            wait_all(gather, other)


def _moe(xn_slabs, tile_expert, num_tiles, tok, dst, w_up, w_down, n_rows_out):
    d, ff2 = w_up.shape[1], w_up.shape[2]
    nrows, _, tm = tok.shape
    sr = _slab_rows(d)
    steps = tile_expert.shape[0]
    assert nrows == steps + 2 and tm % (2 * MOE_COL_CHUNKS) == 0

    def tile_spec(offset):
        return pl.BlockSpec((1, 1, tm), lambda i, te, nt: (i + offset, 0, 0))

    grid_spec = pltpu.PrefetchScalarGridSpec(
        num_scalar_prefetch=2,
        grid=(steps,),
        in_specs=[
            pl.BlockSpec((1, 1, tm), lambda i, te, nt: (1, 0, 0)),
            tile_spec(2),
            tile_spec(0),
            pl.BlockSpec(memory_space=pl.ANY),
            pl.BlockSpec((None, d, ff2), lambda i, te, nt: (te[i], 0, 0)),
            pl.BlockSpec((None, ff2 // 2, d), lambda i, te, nt: (te[i], 0, 0)),
        ],
        out_specs=pl.BlockSpec(memory_space=pl.ANY),
        scratch_shapes=[
            pltpu.SMEM((1, 1, tm), I32),
            pltpu.SMEM((1, 1, tm), I32),
            pltpu.VMEM((2, tm * sr, LANES), F32),
            pltpu.VMEM((2, tm * sr, LANES), F32),
            pltpu.SemaphoreType.DMA((2,)),
            pltpu.SemaphoreType.DMA((1,)),
            pltpu.SemaphoreType.DMA((1,)),
        ],
    )
    return pl.pallas_call(
        _moe_body,
        grid_spec=grid_spec,
        out_shape=jax.ShapeDtypeStruct((n_rows_out * sr, LANES), F32),
        compiler_params=_cparams(("arbitrary",)),
        name="moe_experts",
    )(tile_expert, num_tiles, tok, tok, dst, xn_slabs, w_up, w_down)


def _dispatch(ri, t, sr):
    tm = TM_MOE
    n_assign = TOP_K * t
    ntiles = n_assign // tm + N_EXPERTS
    e_flat = ri[0:TOP_K].reshape(n_assign)
    e_sorted, a_sorted = lax.sort((e_flat, jnp.arange(n_assign, dtype=I32)), num_keys=1)
    bounds = jnp.sum((e_sorted[None, :] < jnp.arange(N_EXPERTS + 1, dtype=I32)[:, None]).astype(I32), axis=1)
    start = bounds[:-1]
    counts = bounds[1:] - start
    tiles_per = (counts + tm - 1) // tm
    tile_end = jnp.cumsum(tiles_per)
    num_tiles = tile_end[-1:].astype(I32)
    tile_ids = jnp.arange(ntiles, dtype=I32)
    tile_expert = jnp.minimum(jnp.sum((tile_ids[:, None] >= tile_end[None, :]).astype(I32), axis=1), N_EXPERTS - 1)
    last_valid = jnp.take(tile_expert, jnp.maximum(num_tiles[0] - 1, 0))
    tile_expert = jnp.where(tile_ids < num_tiles[0], tile_expert, last_valid).astype(I32)
    rows_before = (tile_ids - jnp.take(tile_end - tiles_per, tile_expert)) * tm
    first = jnp.take(start, tile_expert) + rows_before
    n_valid = jnp.where(tile_ids < num_tiles[0], jnp.clip(jnp.take(counts, tile_expert) - rows_before, 0, tm), 0)
    r = jnp.arange(tm, dtype=I32)
    valid = r[None, :] < n_valid[:, None]
    idx = jnp.clip(first[:, None] + r[None, :], 0, n_assign - 1)
    a = jnp.take(a_sorted, idx)
    tok = (jnp.where(valid, a % t, 0) * sr).astype(I32).reshape(ntiles, 1, tm)
    dst = (jnp.where(valid, a, n_assign + r[None, :]) * sr).astype(I32).reshape(ntiles, 1, tm)
    pad_tok = jnp.zeros((1, 1, tm), I32)
    pad_dst = ((n_assign + jnp.arange(tm, dtype=I32)) * sr).reshape(1, 1, tm)
    tok = jnp.concatenate([pad_tok, tok, pad_tok, pad_tok, pad_tok], axis=0)
    dst = jnp.concatenate([pad_dst, dst, pad_dst, pad_dst, pad_dst], axis=0)
    tile_expert = jnp.concatenate([tile_expert, jnp.broadcast_to(last_valid, (2,)).astype(I32)])
    return tile_expert, num_tiles, tok, dst


def _combine(h_ref, y0_ref, y1_ref, rw_ref):
    m, d = h_ref.shape
    rw = rw_ref[...]
    gates = jnp.concatenate([rw, jnp.zeros((LANES - SUBLANES, m), F32)], axis=0).T
    y0 = jnp.concatenate(_load_slabs(y0_ref, m, d), axis=1)
    y1 = jnp.concatenate(_load_slabs(y1_ref, m, d), axis=1)
    return h_ref[...] + gates[:, 0:1] * y0 + gates[:, 1:2] * y1


def _slab_specs(d, nb):
    rows = TM * _slab_rows(d)
    return [pl.BlockSpec((rows, LANES), lambda i: (i, 0)), pl.BlockSpec((rows, LANES), lambda i: (i + nb, 0)),
            pl.BlockSpec((SUBLANES, TM), lambda i: (0, i))]


def _inproj1_body(h_ref, y0_ref, y1_ref, rw_ref, g_ref, w_ref, hn_ref, qkv_ref):
    h = _combine(h_ref, y0_ref, y1_ref, rw_ref)
    d = h.shape[1]
    hn_ref[...] = h
    xb = _rms(h, g_ref[...]).astype(BF16)
    for c in range(w_ref.shape[1] // d):
        qkv_ref[:, c * d:(c + 1) * d] = _dot(xb, w_ref[:, c * d:(c + 1) * d]).astype(BF16)


def _inproj1(h, y2, rw, g, w):
    t, d = h.shape
    n = w.shape[1]
    nb = t // TM
    row_spec = pl.BlockSpec((TM, d), lambda i: (i, 0))
    return pl.pallas_call(
        _inproj1_body,
        grid=(nb,),
        in_specs=[row_spec] + _slab_specs(d, nb) + [
            pl.BlockSpec((1, d), lambda i: (0, 0)),
            pl.BlockSpec((d, n), lambda i: (0, 0)),
        ],
        out_specs=[row_spec, pl.BlockSpec((TM, n), lambda i: (i, 0))],
        out_shape=[jax.ShapeDtypeStruct((t, d), F32), jax.ShapeDtypeStruct((t, n), BF16)],
        compiler_params=_cparams(("parallel",)),
        name="inproj1",
    )(h, y2, y2, rw, g, w)


def _attn_body(q_ref, k_ref, v_ref, lq1_ref, lk1_ref, lq2_ref, lk2_ref, ng_ref, o_ref,
               vaug_s, bias_s, s_s, p_s, m_s, *, heads, lambda_init):
    h = pl.program_id(1)
    seq, dv = q_ref.shape
    nq = seq // TQ
    dqk = dv // 2
    slope = jnp.exp2(jnp.full((1, 1), -8.0 / heads, F32) * (h + 1).astype(F32))
    scale = dqk ** -0.5

    vaug_s[:, 0:dv] = v_ref[...]
    vaug_s[:, dv:2 * dv] = (lax.broadcasted_iota(I32, (seq, dv), 1) == 0).astype(BF16)

    rowp = lax.broadcasted_iota(I32, (2 * TQ, TK), 0) % TQ
    colp = lax.broadcasted_iota(I32, (2 * TQ, TK), 1)
    allowed = (colp // CHUNK) <= (rowp // CHUNK)
    rowf = rowp.astype(F32)
    bias_s[...] = jnp.where(allowed, slope * (rowf - jnp.abs(rowf - colp.astype(F32))), NEG_INF)
    kcol = slope * lax.broadcasted_iota(I32, (1, TK), 1).astype(F32)
    lam = (jnp.exp(jnp.sum(lq1_ref[...] * lk1_ref[...], axis=-1, keepdims=True))
           - jnp.exp(jnp.sum(lq2_ref[...] * lk2_ref[...], axis=-1, keepdims=True)) + lambda_init)
    gain = ng_ref[pl.ds(h, 1), :] * (1.0 - lambda_init)
    lane = lax.broadcasted_iota(I32, (TQ, dv), 1)

    for qi in range(nq):
        q = q_ref[qi * TQ:(qi + 1) * TQ, :].astype(F32) * scale
        qs = jnp.concatenate([jnp.where(lane < dqk, q, 0.0), jnp.where(lane >= dqk, q, 0.0)], axis=0).astype(BF16)
        mlane = None
        for j in range(qi + 1):
            s = _dot_nt(qs, k_ref[j * TK:(j + 1) * TK, :])
            if j < qi:
                s = s + (kcol - slope * float((qi - j) * TK))
            else:
                s = s + bias_s[...]
            s_s[:, j * TK:(j + 1) * TK] = s
            for c in range(TK // LANES):
                part = s[:, c * LANES:(c + 1) * LANES]
                mlane = part if mlane is None else jnp.maximum(mlane, part)
        m_s[...] = jnp.broadcast_to(jnp.max(mlane, axis=-1, keepdims=True), m_s.shape)
        for j in range(qi + 1):
            for c in range(TK // LANES):
                lo = j * TK + c * LANES
                p_s[:, lo:lo + LANES] = jnp.exp(s_s[:, lo:lo + LANES] - m_s[...]).astype(BF16)
        kv = (qi + 1) * TK
        acc = _dot(p_s[:, 0:kv], vaug_s[0:kv, :])
        o1 = acc[0:TQ, 0:dv] / acc[0:TQ, dv:dv + 1]
        o2 = acc[TQ:2 * TQ, 0:dv] / acc[TQ:2 * TQ, dv:dv + 1]
        o_ref[qi * TQ:(qi + 1) * TQ, :] = _rms(o1 - lam * o2, gain).astype(BF16)


def _attn(qkv, lq1, lk1, lq2, lk2, norm_g, batch, seq, heads, dv, lambda_init):
    t = qkv.shape[0]
    dqk = dv // 2
    small = pl.BlockSpec((1, dqk), lambda b, h: (0, 0))
    return pl.pallas_call(
        functools.partial(_attn_body, heads=heads, lambda_init=lambda_init),
        grid=(batch, heads),
        in_specs=[
            pl.BlockSpec((seq, dv), lambda b, h: (b, h)),
            pl.BlockSpec((seq, dv), lambda b, h: (b, heads + h)),
            pl.BlockSpec((seq, dv), lambda b, h: (b, 2 * heads + h)),
            small, small, small, small,
            pl.BlockSpec((heads, dv), lambda b, h: (0, 0)),
        ],
        out_specs=pl.BlockSpec((seq, dv), lambda b, h: (b, h)),
        out_shape=jax.ShapeDtypeStruct((t, heads * dv), BF16),
        scratch_shapes=[
            pltpu.VMEM((seq, 2 * dv), BF16),
            pltpu.VMEM((2 * TQ, TK), F32),
            pltpu.VMEM((2 * TQ, seq), F32),
            pltpu.VMEM((2 * TQ, seq), BF16),
            pltpu.VMEM((2 * TQ, LANES), F32),
        ],
        compiler_params=_cparams(("parallel", "parallel")),
        name="diff_attn",
    )(qkv, qkv, qkv, lq1, lk1, lq2, lk2, norm_g)


def _final_body(h_ref, y0_ref, y1_ref, rw_ref, g_ref, o_ref):
    o_ref[...] = _rms(_combine(h_ref, y0_ref, y1_ref, rw_ref), g_ref[...])


def _final(h, y2, rw, g):
    t, d = h.shape
    nb = t // TM
    row_spec = pl.BlockSpec((TM, d), lambda i: (i, 0))
    return pl.pallas_call(
        _final_body,
        grid=(nb,),
        in_specs=[row_spec] + _slab_specs(d, nb) + [pl.BlockSpec((1, d), lambda i: (0, 0))],
        out_specs=row_spec,
        out_shape=jax.ShapeDtypeStruct((t, d), F32),
        compiler_params=_cparams(("parallel",)),
        name="final_norm",
    )(h, y2, y2, rw, g)


def _block_diag(w):
    nb, bd, _ = w.shape
    eye = jnp.eye(nb, dtype=w.dtype)
    return (eye[:, None, :, None] * w[:, :, None, :]).reshape(nb * bd, nb * bd)


def _router_weights(w_group, b_group, w_expert, b_expert):
    d = w_group.shape[0]
    wr = jnp.zeros((d, LANES), F32)
    wr = wr.at[:, 0:N_GROUPS].set(w_group).at[:, SUBLANES:SUBLANES + N_EXPERTS].set(w_expert)
    br = jnp.zeros((1, LANES), F32)
    br = br.at[0, 0:N_GROUPS].set(b_group).at[0, SUBLANES:SUBLANES + N_EXPERTS].set(b_expert)
    return wr, br


def _moe_layer(xn_slabs, ri, w_up, w_down):
    t = ri.shape[1]
    tile_expert, num_tiles, tok, dst = _dispatch(ri, t, _slab_rows(w_up.shape[1]))
    return _moe(xn_slabs, tile_expert, num_tiles, tok, dst, w_up, w_down, TOP_K * t + TM_MOE)


def kernel(x, norm_mix_g, norm_ffn_g, ab_w_in, ab_w_out, lru_conv_w, lru_conv_b, lru_w_r, lru_b_r, lru_w_i,
           lru_b_i, lru_lambda, mlstm_b_i, mlstm_b_f, mlstm_norm_g, diff_w_qkv, diff_w_o, diff_lambda_q1,
           diff_lambda_k1, diff_lambda_q2, diff_lambda_k2, diff_norm_g, moe_w_group, moe_b_group, moe_w_expert,
           moe_b_expert, moe_w_up, moe_w_down, final_norm_g):
    batch, seq, d = x.shape
    t = batch * seq
    lru_width = lru_lambda.shape[1]
    m_heads, m_dh = mlstm_norm_g.shape[1], mlstm_norm_g.shape[2]
    m_width = m_heads * m_dh
    a_heads, a_dv = diff_norm_g.shape[1], diff_norm_g.shape[2]
    n_main = 2 * lru_width + 4 * m_width
    assert n_main + 2 * m_heads == ab_w_in.shape[2]
    assert seq % TS_LRU == 0 and seq % TQ == 0 and t % TM == 0 and (TOP_K * t) % TM_MOE == 0
    assert 2 * m_heads <= SUBLANES and TQ == TK and TQ % CHUNK == 0

    x2 = x.reshape(t, d)

    w_in = ab_w_in[0]
    w_gate = jnp.zeros((d, LANES), F32).at[:, 0:2 * m_heads].set(w_in[:, n_main:]).astype(BF16)
    b_gate = jnp.zeros((1, LANES), F32).at[0, 0:m_heads].set(mlstm_b_i[0]).at[0, m_heads:2 * m_heads].set(mlstm_b_f[0])
    pf, qkvo, gate = _inproj0(x2, norm_mix_g[0:1], w_in[:, 0:n_main].astype(BF16), w_gate, b_gate)

    w_ri = jnp.concatenate([_block_diag(lru_w_r[0]), _block_diag(lru_w_i[0])], axis=1).astype(BF16)
    b_ri = jnp.concatenate([lru_b_r[0], lru_b_i[0]])[None, :]
    ya = _lru(pf, lru_conv_w[0], lru_conv_b[0][None, :], w_ri, b_ri, lru_lambda[0][None, :], batch, seq)

    grow = gate[:, 0:SUBLANES].reshape(t // CHUNK, CHUNK, SUBLANES).transpose(0, 2, 1)
    yb = _mlstm(qkvo, gate, grow, mlstm_norm_g[0].reshape(1, m_width), batch, seq, m_heads, m_dh)

    w_out = ab_w_out[0].astype(BF16)
    wr, br = _router_weights(moe_w_group[0], moe_b_group[0], moe_w_expert[0], moe_b_expert[0])
    h1, xn, ri, rw = _outproj_router(x2, [ya, yb], [w_out[0:lru_width], w_out[lru_width:]], norm_ffn_g[0:1], wr, br)
    y2 = _moe_layer(xn, ri, moe_w_up[0], moe_w_down[0])

    lambda_init = 0.8 - 0.6 * math.exp(-0.3 * 1)
    h2, qkv = _inproj1(h1, y2, rw, norm_mix_g[1:2], diff_w_qkv[0].astype(BF16))
    attn = _attn(qkv, diff_lambda_q1[0][None, :], diff_lambda_k1[0][None, :], diff_lambda_q2[0][None, :],
                 diff_lambda_k2[0][None, :], diff_norm_g[0], batch, seq, a_heads, a_dv, lambda_init)
    wr, br = _router_weights(moe_w_group[1], moe_b_group[1], moe_w_expert[1], moe_b_expert[1])
    h3, xn, ri, rw = _outproj_router(h2, [attn], [diff_w_o[0].astype(BF16)], norm_ffn_g[1:2], wr, br)
    y2 = _moe_layer(xn, ri, moe_w_up[1], moe_w_down[1])

    return _final(h3, y2, rw, final_norm_g[None, :]).reshape(batch, seq, d)
```

```python
import functools
import math

import jax
import jax.numpy as jnp
from jax import lax
from jax.experimental import pallas as pl
from jax.experimental.pallas import tpu as pltpu

F32 = jnp.float32
BF16 = jnp.bfloat16
I32 = jnp.int32

RMS_EPS = 1e-6
CHUNK = 64
LRU_C = 8.0
N_GROUPS = 4
EXPERTS_PER_GROUP = 8
N_EXPERTS = N_GROUPS * EXPERTS_PER_GROUP
TOP_K = 2

LANES = 128
SUBLANES = 8
VMEM_LIMIT = 56 * 1024 * 1024

TM = 512
TS_LRU = 512
TQ = 256
TK = 256
TM_MOE = 512
MLSTM_SEQS = 2
MOE_COL_CHUNKS = 4

NEG_INF = float("-inf")


def _cparams(sem):
    return pltpu.CompilerParams(dimension_semantics=sem, vmem_limit_bytes=VMEM_LIMIT)


def _dot(a, b, precision=None):
    return jnp.dot(a, b, preferred_element_type=F32, precision=precision)


def _dot_nt(a, b):
    return lax.dot_general(a, b, (((1,), (1,)), ((), ())), preferred_element_type=F32)


def _rms(x, g):
    ms = jnp.mean(x * x, axis=-1, keepdims=True)
    return x * lax.rsqrt(ms + RMS_EPS) * g


def _log_sigmoid(x):
    return jnp.minimum(x, 0.0) - jnp.log1p(jnp.exp(-jnp.abs(x)))


def _sigmoid(x):
    return 1.0 / (1.0 + jnp.exp(-x))


def _slab_rows(d):
    return d // LANES


def _store_slabs(ref, x, lead=()):
    m, d = x.shape
    sr = _slab_rows(d)
    for c in range(sr):
        ref[lead + (pl.ds(c, m, stride=sr), slice(None))] = x[:, c * LANES:(c + 1) * LANES]


def _load_slabs(ref, m, d, lead=()):
    sr = _slab_rows(d)
    return [ref[lead + (pl.ds(c, m, stride=sr), slice(None))] for c in range(sr)]


def _inproj0_body(x_ref, g_ref, w_ref, wg_ref, bg_ref, pf_ref, qkvo_ref, gate_ref):
    d = x_ref.shape[1]
    xb = _rms(x_ref[...], g_ref[...]).astype(BF16)
    pf_ref[...] = _dot(xb, w_ref[:, 0:d])
    for c in range(qkvo_ref.shape[1] // d):
        qkvo_ref[:, c * d:(c + 1) * d] = _dot(xb, w_ref[:, (c + 1) * d:(c + 2) * d]).astype(BF16)
    gate_ref[...] = _dot(xb, wg_ref[...]) + bg_ref[...]


def _inproj0(x2, g, w_main, w_gate, b_gate):
    t, d = x2.shape
    n_main = w_main.shape[1]
    grid = (t // TM,)
    return pl.pallas_call(
        _inproj0_body,
        grid=grid,
        in_specs=[
            pl.BlockSpec((TM, d), lambda i: (i, 0)),
            pl.BlockSpec((1, d), lambda i: (0, 0)),
            pl.BlockSpec((d, n_main), lambda i: (0, 0)),
            pl.BlockSpec((d, LANES), lambda i: (0, 0)),
            pl.BlockSpec((1, LANES), lambda i: (0, 0)),
        ],
        out_specs=[
            pl.BlockSpec((TM, d), lambda i: (i, 0)),
            pl.BlockSpec((TM, n_main - d), lambda i: (i, 0)),
            pl.BlockSpec((TM, LANES), lambda i: (i, 0)),
        ],
        out_shape=[
            jax.ShapeDtypeStruct((t, d), F32),
            jax.ShapeDtypeStruct((t, n_main - d), BF16),
            jax.ShapeDtypeStruct((t, LANES), F32),
        ],
        compiler_params=_cparams(("parallel",)),
        name="inproj0",
    )(x2, g, w_main, w_gate, b_gate)


def _lru_body(pf_ref, cw_ref, cb_ref, wri_ref, bri_ref, lam_ref, ya_ref, xbuf, a_s, u_s, hcar):
    ts = pf_ref.shape[0]
    c = ya_ref.shape[1]
    conv_w = cw_ref.shape[0]
    halo = SUBLANES

    @pl.when(pl.program_id(1) == 0)
    def _():
        xbuf[0:halo, :] = jnp.zeros((halo, c), F32)
        hcar[...] = jnp.zeros_like(hcar)

    xbuf[halo:, :] = pf_ref[:, 0:c]
    xc = cb_ref[...] + cw_ref[0:1, :] * xbuf[pl.ds(halo - conv_w + 1, ts), :]
    for j in range(1, conv_w):
        xc = xc + cw_ref[j:j + 1, :] * xbuf[pl.ds(halo - conv_w + 1 + j, ts), :]
    xbuf[0:halo, :] = xbuf[ts:ts + halo, :]

    ri = _dot(xc.astype(BF16), wri_ref[...]) + bri_ref[...]
    r = _sigmoid(ri[:, 0:c])
    i = _sigmoid(ri[:, c:2 * c])
    log_a = LRU_C * r * _log_sigmoid(lam_ref[...])
    a = jnp.exp(log_a)
    a_s[...] = a
    u_s[...] = jnp.sqrt(-jnp.tanh(log_a) * (a * a + 1.0)) * (i * xc)

    row = lax.broadcasted_iota(I32, (SUBLANES, c), 0)

    def blk(j, carry):
        r0 = pl.multiple_of(j * SUBLANES, SUBLANES)
        a = a_s[pl.ds(r0, SUBLANES), :]
        u = u_s[pl.ds(r0, SUBLANES), :]
        for sh in (1, 2, 4):
            a_sh = jnp.where(row >= sh, pltpu.roll(a, sh, 0), 1.0)
            u_sh = jnp.where(row >= sh, pltpu.roll(u, sh, 0), 0.0)
            u = a * u_sh + u
            a = a * a_sh
        h = a * carry + u
        u_s[pl.ds(r0, SUBLANES), :] = h
        return h[SUBLANES - 1:SUBLANES, :]

    hcar[...] = lax.fori_loop(0, ts // SUBLANES, blk, hcar[...], unroll=2)
    ya_ref[...] = (u_s[...] * jax.nn.gelu(pf_ref[:, c:2 * c])).astype(BF16)


def _lru(pf, conv_w, conv_b, w_ri, b_ri, lam, batch, seq):
    t = pf.shape[0]
    c = lam.shape[1]
    ns = seq // TS_LRU
    return pl.pallas_call(
        _lru_body,
        grid=(batch, ns),
        in_specs=[
            pl.BlockSpec((TS_LRU, 2 * c), lambda b, s: (b * ns + s, 0)),
            pl.BlockSpec(conv_w.shape, lambda b, s: (0, 0)),
            pl.BlockSpec((1, c), lambda b, s: (0, 0)),
            pl.BlockSpec((c, 2 * c), lambda b, s: (0, 0)),
            pl.BlockSpec((1, 2 * c), lambda b, s: (0, 0)),
            pl.BlockSpec((1, c), lambda b, s: (0, 0)),
        ],
        out_specs=pl.BlockSpec((TS_LRU, c), lambda b, s: (b * ns + s, 0)),
        out_shape=jax.ShapeDtypeStruct((t, c), BF16),
        scratch_shapes=[
            pltpu.VMEM((TS_LRU + SUBLANES, c), F32),
            pltpu.VMEM((TS_LRU, c), F32),
            pltpu.VMEM((TS_LRU, c), F32),
            pltpu.VMEM((1, c), F32),
        ],
        compiler_params=_cparams(("parallel", "arbitrary")),
        name="rglru",
    )(pf, conv_w, conv_b, w_ri, b_ri, lam)


def _mlstm_body(qkvo_ref, gate_ref, grow_ref, mg_ref, yb_ref, c_s, m_s, *, heads, dh, nseq):
    seq = qkvo_ref.shape[0] // nseq
    nc = seq // CHUNK
    width = heads * dh
    scale = dh ** -0.5
    hi = lax.Precision.HIGHEST

    c_s[...] = jnp.zeros_like(c_s)
    m_s[...] = jnp.zeros_like(m_s)

    rr = lax.broadcasted_iota(I32, (CHUNK, CHUNK), 0)
    cc = lax.broadcasted_iota(I32, (CHUNK, CHUNK), 1)
    causal = cc <= rr
    tril = causal.astype(F32)
    triu = (rr <= cc).astype(F32)
    ones_col = (lax.broadcasted_iota(I32, (CHUNK, dh), 1) == 0).astype(BF16)

    def step(ci, carry):
        for sq in range(nseq):
            chunk_step(ci, sq)
        return carry

    def chunk_step(ci, sq):
        r0 = pl.multiple_of(sq * seq + ci * CHUNK, CHUNK)
        gcol = gate_ref[pl.ds(r0, CHUNK), :]
        grow = grow_ref[sq * nc + ci]
        bcol = _dot(tril, _log_sigmoid(gcol), precision=hi)
        brow = _dot(_log_sigmoid(grow), triu, precision=hi)
        for h in range(heads):
            st = sq * heads + h
            i_col = gcol[:, h:h + 1]
            b_col = bcol[:, heads + h:heads + h + 1]
            i_row = grow[h:h + 1, :]
            b_row = brow[heads + h:heads + h + 1, :]
            q = qkvo_ref[pl.ds(r0, CHUNK), h * dh:(h + 1) * dh]
            k = qkvo_ref[pl.ds(r0, CHUNK), width + h * dh:width + (h + 1) * dh]
            v = qkvo_ref[pl.ds(r0, CHUNK), 2 * width + h * dh:2 * width + (h + 1) * dh]
            o = qkvo_ref[pl.ds(r0, CHUNK), 3 * width + h * dh:3 * width + (h + 1) * dh]
            m_prev = m_s[st:st + 1, 0:1]
            cst = c_s[st]

            log_d = jnp.where(causal, b_col - b_row + i_row, NEG_INF)
            log_inter = b_col + m_prev
            m_t = jnp.maximum(log_inter, jnp.max(log_d, axis=-1, keepdims=True))
            w_intra = jnp.exp(log_d - m_t)
            w_inter = jnp.exp(log_inter - m_t)
            s = _dot_nt(q, k) * scale * w_intra
            v_aug = jnp.concatenate([v, ones_col], axis=1)
            res = w_inter * _dot(q, cst.astype(BF16)) + _dot(s.astype(BF16), v_aug)
            num = res[:, 0:dh]
            den = res[:, dh:dh + 1]
            hh = num / jnp.maximum(jnp.abs(den), jnp.exp(-m_t))
            y = _rms(hh, mg_ref[:, h * dh:(h + 1) * dh]) * _sigmoid(o.astype(F32))
            yb_ref[pl.ds(r0, CHUNK), h * dh:(h + 1) * dh] = y.astype(BF16)

            g = b_col[CHUNK - 1:CHUNK, :]
            log_w = g - b_col + i_col
            m_new = jnp.maximum(g + m_prev, jnp.max(log_w, axis=0, keepdims=True))
            w_state = jnp.exp(log_w - m_new) * scale
            decay = jnp.exp(g + m_prev - m_new)
            wv = (v_aug.astype(F32) * w_state).astype(BF16)
            upd = _dot(k.astype(F32).T.astype(BF16), wv)
            c_s[st] = decay * cst + upd
            m_s[st:st + 1, 0:1] = m_new

    lax.fori_loop(0, nc, step, 0)


def _mlstm(qkvo, gate, grow, mnorm_g, batch, seq, heads, dh):
    t = qkvo.shape[0]
    width = heads * dh
    nc = seq // CHUNK
    nseq = MLSTM_SEQS if batch % MLSTM_SEQS == 0 else 1
    assert nseq * heads <= SUBLANES
    rows = nseq * seq
    return pl.pallas_call(
        functools.partial(_mlstm_body, heads=heads, dh=dh, nseq=nseq),
        grid=(batch // nseq,),
        in_specs=[
            pl.BlockSpec((rows, 4 * width), lambda b: (b, 0)),
            pl.BlockSpec((rows, LANES), lambda b: (b, 0)),
            pl.BlockSpec((nseq * nc, SUBLANES, CHUNK), lambda b: (b, 0, 0)),
            pl.BlockSpec((1, width), lambda b: (0, 0)),
        ],
        out_specs=pl.BlockSpec((rows, width), lambda b: (b, 0)),
        out_shape=jax.ShapeDtypeStruct((t, width), BF16),
        scratch_shapes=[
            pltpu.VMEM((nseq * heads, dh, 2 * dh), F32),
            pltpu.VMEM((SUBLANES, LANES), F32),
        ],
        compiler_params=_cparams(("parallel",)),
        name="mlstm",
    )(qkvo, gate, grow, mnorm_g)


def _route(logits_t):
    tm = logits_t.shape[1]
    row = lax.broadcasted_iota(I32, (SUBLANES, tm), 0)
    rowf = row.astype(F32)
    none = float(SUBLANES)
    gl = jnp.where(row < N_GROUPS, logits_t[0:SUBLANES], NEG_INF)
    gmax = jnp.max(gl, axis=0, keepdims=True)
    gidx = jnp.min(jnp.where(gl == gmax, rowf, none), axis=0, keepdims=True)
    gweight = 1.0 / jnp.sum(jnp.exp(gl - gmax), axis=0, keepdims=True)
    el = logits_t[SUBLANES:2 * SUBLANES]
    for g in range(1, N_GROUPS):
        el = jnp.where(gidx == float(g), logits_t[(g + 1) * SUBLANES:(g + 2) * SUBLANES], el)
    v1 = jnp.max(el, axis=0, keepdims=True)
    i1 = jnp.min(jnp.where(el == v1, rowf, none), axis=0, keepdims=True)
    el2 = jnp.where(rowf == i1, NEG_INF, el)
    v2 = jnp.max(el2, axis=0, keepdims=True)
    i2 = jnp.min(jnp.where(el2 == v2, rowf, none), axis=0, keepdims=True)
    t = jnp.exp(v2 - v1)
    w1 = gweight / (1.0 + t)
    w2 = gweight * t / (1.0 + t)
    e1 = (gidx * EXPERTS_PER_GROUP + i1).astype(I32)
    e2 = (gidx * EXPERTS_PER_GROUP + i2).astype(I32)
    ids = jnp.where(row == 0, e1, jnp.where(row == 1, e2, 0))
    wts = jnp.where(row == 0, w1, jnp.where(row == 1, w2, 0.0))
    return ids, wts


def _outproj_router_body(*refs, nact):
    h_ref = refs[0]
    act_refs = refs[1:1 + nact]
    w_refs = refs[1 + nact:1 + 2 * nact]
    g_ref, wr_ref, br_ref, hn_ref, xn_ref, ri_ref, rw_ref = refs[1 + 2 * nact:]
    acc = h_ref[...]
    for a_ref, w_ref in zip(act_refs, w_refs):
        acc = acc + _dot(a_ref[...], w_ref[...])
    hn_ref[...] = acc
    xn = _rms(acc, g_ref[...])
    _store_slabs(xn_ref, xn)
    xh = xn.astype(BF16)
    xl = (xn - xh.astype(F32)).astype(BF16)
    part = _dot(xh, wr_ref[...])
    logits = part[:, 0:LANES] + part[:, LANES:2 * LANES] + _dot(xl, wr_ref[:, 0:LANES]) + br_ref[...]
    ids, wts = _route(logits.T)
    ri_ref[...] = ids
    rw_ref[...] = wts


def _outproj_router(h, acts, ws, g, wr, br):
    t, d = h.shape
    nact = len(acts)
    row_spec = pl.BlockSpec((TM, d), lambda i: (i, 0))
    in_specs = [row_spec]
    in_specs += [pl.BlockSpec((TM, a.shape[1]), lambda i: (i, 0)) for a in acts]
    in_specs += [pl.BlockSpec(w.shape, lambda i: (0, 0)) for w in ws]
    in_specs += [
        pl.BlockSpec((1, d), lambda i: (0, 0)),
        pl.BlockSpec((d, 2 * LANES), lambda i: (0, 0)),
        pl.BlockSpec((1, LANES), lambda i: (0, 0)),
    ]
    return pl.pallas_call(
        functools.partial(_outproj_router_body, nact=nact),
        grid=(t // TM,),
        in_specs=in_specs,
        out_specs=[
            row_spec,
            pl.BlockSpec((TM * _slab_rows(d), LANES), lambda i: (i, 0)),
            pl.BlockSpec((SUBLANES, TM), lambda i: (0, i)),
            pl.BlockSpec((SUBLANES, TM), lambda i: (0, i)),
        ],
        out_shape=[
            jax.ShapeDtypeStruct((t, d), F32),
            jax.ShapeDtypeStruct((t * _slab_rows(d), LANES), F32),
            jax.ShapeDtypeStruct((SUBLANES, t), I32),
            jax.ShapeDtypeStruct((SUBLANES, t), F32),
        ],
        compiler_params=_cparams(("parallel",)),
        name="outproj_router",
    )(h, *acts, *ws, g, wr, br)


def _moe_body(te_ref, nt_ref, tok0_ref, tok_ref, dst_ref, xn_hbm, wup_ref, wdn_ref, y_hbm,
              tok_s, dst_s, xbuf, ybuf, sem_i, sem_g, sem_s):
    i = pl.program_id(0)
    nt = nt_ref[0]
    tm = tok_ref.shape[2]
    d, ff2 = wup_ref.shape
    ff = ff2 // 2
    sr = _slab_rows(d)
    slot = i % 2
    other = 1 - slot
    n_pieces = 2 * MOE_COL_CHUNKS
    rows_per_piece = tm // n_pieces

    def gather(r, s):
        return pltpu.make_async_copy(xn_hbm.at[pl.ds(tok_s[0, 0, r], sr), :], xbuf.at[s, pl.ds(r * sr, sr), :],
                                     sem_g.at[0])

    def scatter(r, s):
        return pltpu.make_async_copy(ybuf.at[s, pl.ds(r * sr, sr), :], y_hbm.at[pl.ds(dst_s[0, 0, r], sr), :],
                                     sem_s.at[0])

    def wait_all(copy, s):
        def body(r, c):
            copy(r, s).wait()
            return c
        lax.fori_loop(0, tm, body, 0, unroll=8)

    @pl.when(i == 0)
    def _():
        ybuf[...] = jnp.zeros_like(ybuf)
        pad = pltpu.make_async_copy(ybuf.at[0], y_hbm.at[pl.ds(y_hbm.shape[0] - tm * sr, tm * sr), :], sem_s.at[0])
        pad.start()
        pad.wait()
        c0 = pltpu.make_async_copy(tok0_ref, tok_s, sem_i.at[0])
        c0.start()
        c0.wait()

        def body(r, c):
            gather(r, 0).start()
            return c
        lax.fori_loop(0, tm, body, 0, unroll=8)

    @pl.when(i <= nt + 1)
    def _():
        wait_all(gather, slot)

        @pl.when(i >= 1)
        def _():
            wait_all(scatter, slot)

        ci = pltpu.make_async_copy(tok_ref, tok_s, sem_i.at[0])
        cd = pltpu.make_async_copy(dst_ref, dst_s, sem_i.at[1])
        ci.start()
        cd.start()
        ci.wait()
        cd.wait()

        def issue(piece):
            for r in range(piece * rows_per_piece, (piece + 1) * rows_per_piece):
                gather(r, other).start(priority=r % 2)
                scatter(r, other).start(priority=r % 2)

        x = jnp.concatenate([p.astype(BF16) for p in _load_slabs(xbuf, tm, d, lead=(slot,))], axis=1)
        cw = ff2 // MOE_COL_CHUNKS
        parts = []
        for c in range(MOE_COL_CHUNKS):
            issue(c)
            parts.append(_dot(x, wup_ref[:, c * cw:(c + 1) * cw].astype(BF16)))
        hcat = jnp.concatenate(parts, axis=1)
        hh = (jax.nn.silu(hcat[:, 0:ff]) * hcat[:, ff:ff2]).astype(BF16)
        cw = d // MOE_COL_CHUNKS
        parts = []
        for c in range(MOE_COL_CHUNKS):
            issue(MOE_COL_CHUNKS + c)
            parts.append(_dot(hh, wdn_ref[:, c * cw:(c + 1) * cw].astype(BF16)))
        _store_slabs(ybuf, jnp.concatenate(parts, axis=1), lead=(slot,))

        @pl.when(i == nt + 1)
        def _():
            wait_all(scatter, other)
            wait_all(gather, other)


def _moe(xn_slabs, tile_expert, num_tiles, tok, dst, w_up, w_down, layer, n_rows_out):
    d, ff2 = w_up.shape[2], w_up.shape[3]
    nrows, _, tm = tok.shape
    sr = _slab_rows(d)
    steps = tile_expert.shape[0]
    assert nrows == steps + 2 and tm % (2 * MOE_COL_CHUNKS) == 0

    def tile_spec(offset):
        return pl.BlockSpec((1, 1, tm), lambda i, te, nt: (i + offset, 0, 0))

    grid_spec = pltpu.PrefetchScalarGridSpec(
        num_scalar_prefetch=2,
        grid=(steps,),
        in_specs=[
            pl.BlockSpec((1, 1, tm), lambda i, te, nt: (1, 0, 0)),
            tile_spec(2),
            tile_spec(0),
            pl.BlockSpec(memory_space=pl.ANY),
            pl.BlockSpec((None, None, d, ff2), lambda i, te, nt: (layer, te[i], 0, 0)),
            pl.BlockSpec((None, None, ff2 // 2, d), lambda i, te, nt: (layer, te[i], 0, 0)),
        ],
        out_specs=pl.BlockSpec(memory_space=pl.ANY),
        scratch_shapes=[
            pltpu.SMEM((1, 1, tm), I32),
            pltpu.SMEM((1, 1, tm), I32),
            pltpu.VMEM((2, tm * sr, LANES), F32),
            pltpu.VMEM((2, tm * sr, LANES), F32),
            pltpu.SemaphoreType.DMA((2,)),
            pltpu.SemaphoreType.DMA((1,)),
            pltpu.SemaphoreType.DMA((1,)),
        ],
    )
    return pl.pallas_call(
        _moe_body,
        grid_spec=grid_spec,
        out_shape=jax.ShapeDtypeStruct((n_rows_out * sr, LANES), F32),
        compiler_params=_cparams(("arbitrary",)),
        name="moe_experts",
    )(tile_expert, num_tiles, tok, tok, dst, xn_slabs, w_up, w_down)


def _dispatch(ri, t, sr):
    tm = TM_MOE
    n_assign = TOP_K * t
    ntiles = n_assign // tm + N_EXPERTS
    e_flat = ri[0:TOP_K].reshape(n_assign)
    e_sorted, a_sorted = lax.sort((e_flat, jnp.arange(n_assign, dtype=I32)), num_keys=1)
    bounds = jnp.sum((e_sorted[None, :] < jnp.arange(N_EXPERTS + 1, dtype=I32)[:, None]).astype(I32), axis=1)
    start = bounds[:-1]
    counts = bounds[1:] - start
    tiles_per = (counts + tm - 1) // tm
    tile_end = jnp.cumsum(tiles_per)
    num_tiles = tile_end[-1:].astype(I32)
    tile_ids = jnp.arange(ntiles, dtype=I32)
    tile_expert = jnp.minimum(jnp.sum((tile_ids[:, None] >= tile_end[None, :]).astype(I32), axis=1), N_EXPERTS - 1)
    last_valid = jnp.take(tile_expert, jnp.maximum(num_tiles[0] - 1, 0))
    tile_expert = jnp.where(tile_ids < num_tiles[0], tile_expert, last_valid).astype(I32)
    rows_before = (tile_ids - jnp.take(tile_end - tiles_per, tile_expert)) * tm
    first = jnp.take(start, tile_expert) + rows_before
    n_valid = jnp.where(tile_ids < num_tiles[0], jnp.clip(jnp.take(counts, tile_expert) - rows_before, 0, tm), 0)
    r = jnp.arange(tm, dtype=I32)
    valid = r[None, :] < n_valid[:, None]
    idx = jnp.clip(first[:, None] + r[None, :], 0, n_assign - 1)
    a = jnp.take(a_sorted, idx)
    tok = (jnp.where(valid, a % t, 0) * sr).astype(I32).reshape(ntiles, 1, tm)
    dst = (jnp.where(valid, a, n_assign + r[None, :]) * sr).astype(I32).reshape(ntiles, 1, tm)
    pad_tok = jnp.zeros((1, 1, tm), I32)
    pad_dst = ((n_assign + jnp.arange(tm, dtype=I32)) * sr).reshape(1, 1, tm)
    tok = jnp.concatenate([pad_tok, tok, pad_tok, pad_tok, pad_tok], axis=0)
    dst = jnp.concatenate([pad_dst, dst, pad_dst, pad_dst, pad_dst], axis=0)
    tile_expert = jnp.concatenate([tile_expert, jnp.broadcast_to(last_valid, (2,)).astype(I32)])
    return tile_expert, num_tiles, tok, dst


def _combine(h_ref, y0_ref, y1_ref, rw_ref):
    m, d = h_ref.shape
    rw = rw_ref[...]
    gates = jnp.concatenate([rw, jnp.zeros((LANES - SUBLANES, m), F32)], axis=0).T
    y0 = jnp.concatenate(_load_slabs(y0_ref, m, d), axis=1)
    y1 = jnp.concatenate(_load_slabs(y1_ref, m, d), axis=1)
    return h_ref[...] + gates[:, 0:1] * y0 + gates[:, 1:2] * y1


def _slab_specs(d, nb):
    rows = TM * _slab_rows(d)
    return [pl.BlockSpec((rows, LANES), lambda i: (i, 0)), pl.BlockSpec((rows, LANES), lambda i: (i + nb, 0)),
            pl.BlockSpec((SUBLANES, TM), lambda i: (0, i))]


def _inproj1_body(h_ref, y0_ref, y1_ref, rw_ref, g_ref, w_ref, hn_ref, qkv_ref):
    h = _combine(h_ref, y0_ref, y1_ref, rw_ref)
    d = h.shape[1]
    hn_ref[...] = h
    xb = _rms(h, g_ref[...]).astype(BF16)
    for c in range(w_ref.shape[1] // d):
        qkv_ref[:, c * d:(c + 1) * d] = _dot(xb, w_ref[:, c * d:(c + 1) * d]).astype(BF16)


def _inproj1(h, y2, rw, g, w):
    t, d = h.shape
    n = w.shape[1]
    nb = t // TM
    row_spec = pl.BlockSpec((TM, d), lambda i: (i, 0))
    return pl.pallas_call(
        _inproj1_body,
        grid=(nb,),
        in_specs=[row_spec] + _slab_specs(d, nb) + [
            pl.BlockSpec((1, d), lambda i: (0, 0)),
            pl.BlockSpec((d, n), lambda i: (0, 0)),
        ],
        out_specs=[row_spec, pl.BlockSpec((TM, n), lambda i: (i, 0))],
        out_shape=[jax.ShapeDtypeStruct((t, d), F32), jax.ShapeDtypeStruct((t, n), BF16)],
        compiler_params=_cparams(("parallel",)),
        name="inproj1",
    )(h, y2, y2, rw, g, w)


def _attn_body(q_ref, k_ref, v_ref, lq1_ref, lk1_ref, lq2_ref, lk2_ref, ng_ref, o_ref,
               vaug_s, bias_s, s_s, p_s, m_s, *, heads, lambda_init):
    h = pl.program_id(1)
    seq, dv = q_ref.shape
    nq = seq // TQ
    dqk = dv // 2
    slope = jnp.exp2(jnp.full((1, 1), -8.0 / heads, F32) * (h + 1).astype(F32))
    scale = dqk ** -0.5

    vaug_s[:, 0:dv] = v_ref[...]
    vaug_s[:, dv:2 * dv] = (lax.broadcasted_iota(I32, (seq, dv), 1) == 0).astype(BF16)

    rowp = lax.broadcasted_iota(I32, (2 * TQ, TK), 0) % TQ
    colp = lax.broadcasted_iota(I32, (2 * TQ, TK), 1)
    allowed = (colp // CHUNK) <= (rowp // CHUNK)
    rowf = rowp.astype(F32)
    bias_s[...] = jnp.where(allowed, slope * (rowf - jnp.abs(rowf - colp.astype(F32))), NEG_INF)
    kcol = slope * lax.broadcasted_iota(I32, (1, TK), 1).astype(F32)
    lam = (jnp.exp(jnp.sum(lq1_ref[...] * lk1_ref[...], axis=-1, keepdims=True))
           - jnp.exp(jnp.sum(lq2_ref[...] * lk2_ref[...], axis=-1, keepdims=True)) + lambda_init)
    gain = ng_ref[pl.ds(h, 1), :] * (1.0 - lambda_init)
    lane = lax.broadcasted_iota(I32, (TQ, dv), 1)

    for qi in range(nq):
        q = q_ref[qi * TQ:(qi + 1) * TQ, :].astype(F32) * scale
        qs = jnp.concatenate([jnp.where(lane < dqk, q, 0.0), jnp.where(lane >= dqk, q, 0.0)], axis=0).astype(BF16)
        mlane = None
        for j in range(qi + 1):
            s = _dot_nt(qs, k_ref[j * TK:(j + 1) * TK, :])
            if j < qi:
                s = s + (kcol - slope * float((qi - j) * TK))
            else:
                s = s + bias_s[...]
            s_s[:, j * TK:(j + 1) * TK] = s
            for c in range(TK // LANES):
                part = s[:, c * LANES:(c + 1) * LANES]
                mlane = part if mlane is None else jnp.maximum(mlane, part)
        m_s[...] = jnp.broadcast_to(jnp.max(mlane, axis=-1, keepdims=True), m_s.shape)
        for j in range(qi + 1):
            for c in range(TK // LANES):
                lo = j * TK + c * LANES
                p_s[:, lo:lo + LANES] = jnp.exp(s_s[:, lo:lo + LANES] - m_s[...]).astype(BF16)
        kv = (qi + 1) * TK
        acc = _dot(p_s[:, 0:kv], vaug_s[0:kv, :])
        o1 = acc[0:TQ, 0:dv] / acc[0:TQ, dv:dv + 1]
        o2 = acc[TQ:2 * TQ, 0:dv] / acc[TQ:2 * TQ, dv:dv + 1]
        o_ref[qi * TQ:(qi + 1) * TQ, :] = _rms(o1 - lam * o2, gain).astype(BF16)


def _attn(qkv, lq1, lk1, lq2, lk2, norm_g, batch, seq, heads, dv, lambda_init):
    t = qkv.shape[0]
    dqk = dv // 2
    small = pl.BlockSpec((1, dqk), lambda b, h: (0, 0))
    return pl.pallas_call(
        functools.partial(_attn_body, heads=heads, lambda_init=lambda_init),
        grid=(batch, heads),
        in_specs=[
            pl.BlockSpec((seq, dv), lambda b, h: (b, h)),
            pl.BlockSpec((seq, dv), lambda b, h: (b, heads + h)),
            pl.BlockSpec((seq, dv), lambda b, h: (b, 2 * heads + h)),
            small, small, small, small,
            pl.BlockSpec((heads, dv), lambda b, h: (0, 0)),
        ],
        out_specs=pl.BlockSpec((seq, dv), lambda b, h: (b, h)),
        out_shape=jax.ShapeDtypeStruct((t, heads * dv), BF16),
        scratch_shapes=[
            pltpu.VMEM((seq, 2 * dv), BF16),
            pltpu.VMEM((2 * TQ, TK), F32),
            pltpu.VMEM((2 * TQ, seq), F32),
            pltpu.VMEM((2 * TQ, seq), BF16),
            pltpu.VMEM((2 * TQ, LANES), F32),
        ],
        compiler_params=_cparams(("parallel", "parallel")),
        name="diff_attn",
    )(qkv, qkv, qkv, lq1, lk1, lq2, lk2, norm_g)


def _final_body(h_ref, y0_ref, y1_ref, rw_ref, g_ref, o_ref):
    o_ref[...] = _rms(_combine(h_ref, y0_ref, y1_ref, rw_ref), g_ref[...])


def _final(h, y2, rw, g):
    t, d = h.shape
    nb = t // TM
    row_spec = pl.BlockSpec((TM, d), lambda i: (i, 0))
    return pl.pallas_call(
        _final_body,
        grid=(nb,),
        in_specs=[row_spec] + _slab_specs(d, nb) + [pl.BlockSpec((1, d), lambda i: (0, 0))],
        out_specs=row_spec,
        out_shape=jax.ShapeDtypeStruct((t, d), F32),
        compiler_params=_cparams(("parallel",)),
        name="final_norm",
    )(h, y2, y2, rw, g)


def _block_diag(w):
    nb, bd, _ = w.shape
    eye = jnp.eye(nb, dtype=w.dtype)
    return (eye[:, None, :, None] * w[:, :, None, :]).reshape(nb * bd, nb * bd)


def _router_weights(w_group, b_group, w_expert, b_expert):
    d = w_group.shape[0]
    wr = jnp.zeros((d, LANES), F32)
    wr = wr.at[:, 0:N_GROUPS].set(w_group).at[:, SUBLANES:SUBLANES + N_EXPERTS].set(w_expert)
    br = jnp.zeros((1, LANES), F32)
    br = br.at[0, 0:N_GROUPS].set(b_group).at[0, SUBLANES:SUBLANES + N_EXPERTS].set(b_expert)
    wr_hi = wr.astype(BF16)
    wr_lo = (wr - wr_hi.astype(F32)).astype(BF16)
    return jnp.concatenate([wr_hi, wr_lo], axis=1), br


def _moe_layer(xn_slabs, ri, w_up, w_down, layer):
    t = ri.shape[1]
    tile_expert, num_tiles, tok, dst = _dispatch(ri, t, _slab_rows(w_up.shape[2]))
    return _moe(xn_slabs, tile_expert, num_tiles, tok, dst, w_up, w_down, layer, TOP_K * t + TM_MOE)


def kernel(x, norm_mix_g, norm_ffn_g, ab_w_in, ab_w_out, lru_conv_w, lru_conv_b, lru_w_r, lru_b_r, lru_w_i,
           lru_b_i, lru_lambda, mlstm_b_i, mlstm_b_f, mlstm_norm_g, diff_w_qkv, diff_w_o, diff_lambda_q1,
           diff_lambda_k1, diff_lambda_q2, diff_lambda_k2, diff_norm_g, moe_w_group, moe_b_group, moe_w_expert,
           moe_b_expert, moe_w_up, moe_w_down, final_norm_g):
    batch, seq, d = x.shape
    t = batch * seq
    lru_width = lru_lambda.shape[1]
    m_heads, m_dh = mlstm_norm_g.shape[1], mlstm_norm_g.shape[2]
    m_width = m_heads * m_dh
    a_heads, a_dv = diff_norm_g.shape[1], diff_norm_g.shape[2]
    n_main = 2 * lru_width + 4 * m_width
    assert n_main + 2 * m_heads == ab_w_in.shape[2]
    assert seq % TS_LRU == 0 and seq % TQ == 0 and t % TM == 0 and (TOP_K * t) % TM_MOE == 0
    assert 2 * m_heads <= SUBLANES and TQ == TK and TQ % CHUNK == 0

    x2 = x.reshape(t, d)

    w_in = ab_w_in[0]
    w_gate = jnp.zeros((d, LANES), F32).at[:, 0:2 * m_heads].set(w_in[:, n_main:]).astype(BF16)
    b_gate = jnp.zeros((1, LANES), F32).at[0, 0:m_heads].set(mlstm_b_i[0]).at[0, m_heads:2 * m_heads].set(mlstm_b_f[0])
    pf, qkvo, gate = _inproj0(x2, norm_mix_g[0:1], w_in[:, 0:n_main].astype(BF16), w_gate, b_gate)

    w_ri = jnp.concatenate([_block_diag(lru_w_r[0]), _block_diag(lru_w_i[0])], axis=1).astype(BF16)
    b_ri = jnp.concatenate([lru_b_r[0], lru_b_i[0]])[None, :]
    ya = _lru(pf, lru_conv_w[0], lru_conv_b[0][None, :], w_ri, b_ri, lru_lambda[0][None, :], batch, seq)

    grow = gate[:, 0:SUBLANES].reshape(t // CHUNK, CHUNK, SUBLANES).transpose(0, 2, 1)
    yb = _mlstm(qkvo, gate, grow, mlstm_norm_g[0].reshape(1, m_width), batch, seq, m_heads, m_dh)

    w_out = ab_w_out[0].astype(BF16)
    wr, br = _router_weights(moe_w_group[0], moe_b_group[0], moe_w_expert[0], moe_b_expert[0])
    h1, xn, ri, rw = _outproj_router(x2, [ya, yb], [w_out[0:lru_width], w_out[lru_width:]], norm_ffn_g[0:1], wr, br)
    y2 = _moe_layer(xn, ri, moe_w_up, moe_w_down, 0)

    lambda_init = 0.8 - 0.6 * math.exp(-0.3 * 1)
    h2, qkv = _inproj1(h1, y2, rw, norm_mix_g[1:2], diff_w_qkv[0].astype(BF16))
    attn = _attn(qkv, diff_lambda_q1[0][None, :], diff_lambda_k1[0][None, :], diff_lambda_q2[0][None, :],
                 diff_lambda_k2[0][None, :], diff_norm_g[0], batch, seq, a_heads, a_dv, lambda_init)
    wr, br = _router_weights(moe_w_group[1], moe_b_group[1], moe_w_expert[1], moe_b_expert[1])
    h3, xn, ri, rw = _outproj_router(h2, [attn], [diff_w_o[0].astype(BF16)], norm_ffn_g[1:2], wr, br)
    y2 = _moe_layer(xn, ri, moe_w_up, moe_w_down, 1)

    return _final(h3, y2, rw, final_norm_g[None, :]).reshape(batch, seq, d)
```

```python
import functools
import math

import jax
import jax.numpy as jnp
from jax import lax
from jax.experimental import pallas as pl
from jax.experimental.pallas import tpu as pltpu

F32 = jnp.float32
BF16 = jnp.bfloat16
I32 = jnp.int32

RMS_EPS = 1e-6
CHUNK = 64
LRU_C = 8.0
N_GROUPS = 4
EXPERTS_PER_GROUP = 8
N_EXPERTS = N_GROUPS * EXPERTS_PER_GROUP
TOP_K = 2

LANES = 128
SUBLANES = 8
VMEM_LIMIT = 56 * 1024 * 1024

TM = 512
TS_LRU = 512
TQ = 256
TK = 256
TM_MOE = 512
MLSTM_SEQS = 2
MOE_COL_CHUNKS = 4

NEG_INF = float("-inf")


def _cparams(sem):
    return pltpu.CompilerParams(dimension_semantics=sem, vmem_limit_bytes=VMEM_LIMIT)


def _dot(a, b, precision=None):
    return jnp.dot(a, b, preferred_element_type=F32, precision=precision)


def _dot_nt(a, b):
    return lax.dot_general(a, b, (((1,), (1,)), ((), ())), preferred_element_type=F32)


def _rms(x, g):
    ms = jnp.mean(x * x, axis=-1, keepdims=True)
    return x * lax.rsqrt(ms + RMS_EPS) * g


def _log_sigmoid(x):
    return jnp.minimum(x, 0.0) - jnp.log1p(jnp.exp(-jnp.abs(x)))


def _sigmoid(x):
    return 1.0 / (1.0 + jnp.exp(-x))


def _slab_rows(d):
    return d // LANES


def _store_slabs(ref, x, lead=()):
    m, d = x.shape
    sr = _slab_rows(d)
    for c in range(sr):
        ref[lead + (pl.ds(c, m, stride=sr), slice(None))] = x[:, c * LANES:(c + 1) * LANES]


def _load_slabs(ref, m, d, lead=()):
    sr = _slab_rows(d)
    return [ref[lead + (pl.ds(c, m, stride=sr), slice(None))] for c in range(sr)]


def _inproj0_body(x_ref, g_ref, w_ref, wg_ref, bg_ref, pf_ref, qkvo_ref, gate_ref):
    d = x_ref.shape[1]
    xb = _rms(x_ref[...], g_ref[...]).astype(BF16)
    pf_ref[...] = _dot(xb, w_ref[:, 0:d])
    for c in range(qkvo_ref.shape[1] // d):
        qkvo_ref[:, c * d:(c + 1) * d] = _dot(xb, w_ref[:, (c + 1) * d:(c + 2) * d]).astype(BF16)
    gate_ref[...] = _dot(xb, wg_ref[...]) + bg_ref[...]


def _inproj0(x2, g, w_main, w_gate, b_gate):
    t, d = x2.shape
    n_main = w_main.shape[1]
    grid = (t // TM,)
    return pl.pallas_call(
        _inproj0_body,
        grid=grid,
        in_specs=[
            pl.BlockSpec((TM, d), lambda i: (i, 0)),
            pl.BlockSpec((1, d), lambda i: (0, 0)),
            pl.BlockSpec((d, n_main), lambda i: (0, 0)),
            pl.BlockSpec((d, LANES), lambda i: (0, 0)),
            pl.BlockSpec((1, LANES), lambda i: (0, 0)),
        ],
        out_specs=[
            pl.BlockSpec((TM, d), lambda i: (i, 0)),
            pl.BlockSpec((TM, n_main - d), lambda i: (i, 0)),
            pl.BlockSpec((TM, LANES), lambda i: (i, 0)),
        ],
        out_shape=[
            jax.ShapeDtypeStruct((t, d), F32),
            jax.ShapeDtypeStruct((t, n_main - d), BF16),
            jax.ShapeDtypeStruct((t, LANES), F32),
        ],
        compiler_params=_cparams(("parallel",)),
        name="inproj0",
    )(x2, g, w_main, w_gate, b_gate)


def _lru_body(pf_ref, cw_ref, cb_ref, wri_ref, bri_ref, lam_ref, ya_ref, xbuf, a_s, u_s, hcar):
    ts = pf_ref.shape[0]
    c = ya_ref.shape[1]
    conv_w = cw_ref.shape[0]
    halo = SUBLANES

    @pl.when(pl.program_id(1) == 0)
    def _():
        xbuf[0:halo, :] = jnp.zeros((halo, c), F32)
        hcar[...] = jnp.zeros_like(hcar)

    xbuf[halo:, :] = pf_ref[:, 0:c]
    xc = cb_ref[...] + cw_ref[0:1, :] * xbuf[pl.ds(halo - conv_w + 1, ts), :]
    for j in range(1, conv_w):
        xc = xc + cw_ref[j:j + 1, :] * xbuf[pl.ds(halo - conv_w + 1 + j, ts), :]
    xbuf[0:halo, :] = xbuf[ts:ts + halo, :]

    ri = _dot(xc.astype(BF16), wri_ref[...]) + bri_ref[...]
    r = _sigmoid(ri[:, 0:c])
    i = _sigmoid(ri[:, c:2 * c])
    log_a = LRU_C * r * _log_sigmoid(lam_ref[...])
    a = jnp.exp(log_a)
    a_s[...] = a
    u_s[...] = jnp.sqrt(-jnp.tanh(log_a) * (a * a + 1.0)) * (i * xc)

    row = lax.broadcasted_iota(I32, (SUBLANES, c), 0)

    def blk(j, carry):
        r0 = pl.multiple_of(j * SUBLANES, SUBLANES)
        a = a_s[pl.ds(r0, SUBLANES), :]
        u = u_s[pl.ds(r0, SUBLANES), :]
        for sh in (1, 2, 4):
            a_sh = jnp.where(row >= sh, pltpu.roll(a, sh, 0), 1.0)
            u_sh = jnp.where(row >= sh, pltpu.roll(u, sh, 0), 0.0)
            u = a * u_sh + u
            a = a * a_sh
        h = a * carry + u
        u_s[pl.ds(r0, SUBLANES), :] = h
        return h[SUBLANES - 1:SUBLANES, :]

    hcar[...] = lax.fori_loop(0, ts // SUBLANES, blk, hcar[...], unroll=2)
    ya_ref[...] = (u_s[...] * jax.nn.gelu(pf_ref[:, c:2 * c])).astype(BF16)


def _lru(pf, conv_w, conv_b, w_ri, b_ri, lam, batch, seq):
    t = pf.shape[0]
    c = lam.shape[1]
    ns = seq // TS_LRU
    return pl.pallas_call(
        _lru_body,
        grid=(batch, ns),
        in_specs=[
            pl.BlockSpec((TS_LRU, 2 * c), lambda b, s: (b * ns + s, 0)),
            pl.BlockSpec(conv_w.shape, lambda b, s: (0, 0)),
            pl.BlockSpec((1, c), lambda b, s: (0, 0)),
            pl.BlockSpec((c, 2 * c), lambda b, s: (0, 0)),
            pl.BlockSpec((1, 2 * c), lambda b, s: (0, 0)),
            pl.BlockSpec((1, c), lambda b, s: (0, 0)),
        ],
        out_specs=pl.BlockSpec((TS_LRU, c), lambda b, s: (b * ns + s, 0)),
        out_shape=jax.ShapeDtypeStruct((t, c), BF16),
        scratch_shapes=[
            pltpu.VMEM((TS_LRU + SUBLANES, c), F32),
            pltpu.VMEM((TS_LRU, c), F32),
            pltpu.VMEM((TS_LRU, c), F32),
            pltpu.VMEM((1, c), F32),
        ],
        compiler_params=_cparams(("parallel", "arbitrary")),
        name="rglru",
    )(pf, conv_w, conv_b, w_ri, b_ri, lam)


def _mlstm_body(qkvo_ref, gate_ref, grow_ref, mg_ref, yb_ref, c_s, m_s, *, heads, dh, nseq):
    seq = qkvo_ref.shape[0] // nseq
    nc = seq // CHUNK
    width = heads * dh
    scale = dh ** -0.5
    hi = lax.Precision.HIGHEST

    c_s[...] = jnp.zeros_like(c_s)
    m_s[...] = jnp.zeros_like(m_s)

    rr = lax.broadcasted_iota(I32, (CHUNK, CHUNK), 0)
    cc = lax.broadcasted_iota(I32, (CHUNK, CHUNK), 1)
    causal = cc <= rr
    tril = causal.astype(F32)
    triu = (rr <= cc).astype(F32)
    ones_col = (lax.broadcasted_iota(I32, (CHUNK, dh), 1) == 0).astype(BF16)

    def step(ci, carry):
        for sq in range(nseq):
            chunk_step(ci, sq)
        return carry

    def chunk_step(ci, sq):
        r0 = pl.multiple_of(sq * seq + ci * CHUNK, CHUNK)
        gcol = gate_ref[pl.ds(r0, CHUNK), :]
        grow = grow_ref[sq * nc + ci]
        bcol = _dot(tril, _log_sigmoid(gcol), precision=hi)
        brow = _dot(_log_sigmoid(grow), triu, precision=hi)
        for h in range(heads):
            st = sq * heads + h
            i_col = gcol[:, h:h + 1]
            b_col = bcol[:, heads + h:heads + h + 1]
            i_row = grow[h:h + 1, :]
            b_row = brow[heads + h:heads + h + 1, :]
            q = qkvo_ref[pl.ds(r0, CHUNK), h * dh:(h + 1) * dh]
            k = qkvo_ref[pl.ds(r0, CHUNK), width + h * dh:width + (h + 1) * dh]
            v = qkvo_ref[pl.ds(r0, CHUNK), 2 * width + h * dh:2 * width + (h + 1) * dh]
            o = qkvo_ref[pl.ds(r0, CHUNK), 3 * width + h * dh:3 * width + (h + 1) * dh]
            m_prev = m_s[st:st + 1, 0:1]
            cst = c_s[st]

            log_d = jnp.where(causal, b_col - b_row + i_row, NEG_INF)
            log_inter = b_col + m_prev
            m_t = jnp.maximum(log_inter, jnp.max(log_d, axis=-1, keepdims=True))
            w_intra = jnp.exp(log_d - m_t)
            w_inter = jnp.exp(log_inter - m_t)
            s = _dot_nt(q, k) * scale * w_intra
            v_aug = jnp.concatenate([v, ones_col], axis=1)
            res = w_inter * _dot(q, cst.astype(BF16)) + _dot(s.astype(BF16), v_aug)
            num = res[:, 0:dh]
            den = res[:, dh:dh + 1]
            hh = num / jnp.maximum(jnp.abs(den), jnp.exp(-m_t))
            y = _rms(hh, mg_ref[:, h * dh:(h + 1) * dh]) * _sigmoid(o.astype(F32))
            yb_ref[pl.ds(r0, CHUNK), h * dh:(h + 1) * dh] = y.astype(BF16)

            g = b_col[CHUNK - 1:CHUNK, :]
            log_w = g - b_col + i_col
            m_new = jnp.maximum(g + m_prev, jnp.max(log_w, axis=0, keepdims=True))
            w_state = jnp.exp(log_w - m_new) * scale
            decay = jnp.exp(g + m_prev - m_new)
            wv = (v_aug.astype(F32) * w_state).astype(BF16)
            upd = _dot(k.astype(F32).T.astype(BF16), wv)
            c_s[st] = decay * cst + upd
            m_s[st:st + 1, 0:1] = m_new

    lax.fori_loop(0, nc, step, 0)


def _mlstm(qkvo, gate, grow, mnorm_g, batch, seq, heads, dh):
    t = qkvo.shape[0]
    width = heads * dh
    nc = seq // CHUNK
    nseq = MLSTM_SEQS if batch % MLSTM_SEQS == 0 else 1
    assert nseq * heads <= SUBLANES
    rows = nseq * seq
    return pl.pallas_call(
        functools.partial(_mlstm_body, heads=heads, dh=dh, nseq=nseq),
        grid=(batch // nseq,),
        in_specs=[
            pl.BlockSpec((rows, 4 * width), lambda b: (b, 0)),
            pl.BlockSpec((rows, LANES), lambda b: (b, 0)),
            pl.BlockSpec((nseq * nc, SUBLANES, CHUNK), lambda b: (b, 0, 0)),
            pl.BlockSpec((1, width), lambda b: (0, 0)),
        ],
        out_specs=pl.BlockSpec((rows, width), lambda b: (b, 0)),
        out_shape=jax.ShapeDtypeStruct((t, width), BF16),
        scratch_shapes=[
            pltpu.VMEM((nseq * heads, dh, 2 * dh), F32),
            pltpu.VMEM((SUBLANES, LANES), F32),
        ],
        compiler_params=_cparams(("parallel",)),
        name="mlstm",
    )(qkvo, gate, grow, mnorm_g)


def _route(logits_t):
    tm = logits_t.shape[1]
    row = lax.broadcasted_iota(I32, (SUBLANES, tm), 0)
    rowf = row.astype(F32)
    none = float(SUBLANES)
    gl = jnp.where(row < N_GROUPS, logits_t[0:SUBLANES], NEG_INF)
    gmax = jnp.max(gl, axis=0, keepdims=True)
    gidx = jnp.min(jnp.where(gl == gmax, rowf, none), axis=0, keepdims=True)
    gweight = 1.0 / jnp.sum(jnp.exp(gl - gmax), axis=0, keepdims=True)
    el = logits_t[SUBLANES:2 * SUBLANES]
    for g in range(1, N_GROUPS):
        el = jnp.where(gidx == float(g), logits_t[(g + 1) * SUBLANES:(g + 2) * SUBLANES], el)
    v1 = jnp.max(el, axis=0, keepdims=True)
    i1 = jnp.min(jnp.where(el == v1, rowf, none), axis=0, keepdims=True)
    el2 = jnp.where(rowf == i1, NEG_INF, el)
    v2 = jnp.max(el2, axis=0, keepdims=True)
    i2 = jnp.min(jnp.where(el2 == v2, rowf, none), axis=0, keepdims=True)
    t = jnp.exp(v2 - v1)
    w1 = gweight / (1.0 + t)
    w2 = gweight * t / (1.0 + t)
    e1 = (gidx * EXPERTS_PER_GROUP + i1).astype(I32)
    e2 = (gidx * EXPERTS_PER_GROUP + i2).astype(I32)
    ids = jnp.where(row == 0, e1, jnp.where(row == 1, e2, 0))
    wts = jnp.where(row == 0, w1, jnp.where(row == 1, w2, 0.0))
    return ids, wts


def _outproj_router_body(*refs, nact):
    h_ref = refs[0]
    act_refs = refs[1:1 + nact]
    w_refs = refs[1 + nact:1 + 2 * nact]
    g_ref, wr_ref, br_ref, hn_ref, xn_ref, ri_ref, rw_ref = refs[1 + 2 * nact:]
    acc = h_ref[...]
    for a_ref, w_ref in zip(act_refs, w_refs):
        acc = acc + _dot(a_ref[...], w_ref[...])
    hn_ref[...] = acc
    xn = _rms(acc, g_ref[...])
    _store_slabs(xn_ref, xn)
    xh = xn.astype(BF16)
    xl = (xn - xh.astype(F32)).astype(BF16)
    part = _dot(xh, wr_ref[...])
    logits = part[:, 0:LANES] + part[:, LANES:2 * LANES] + _dot(xl, wr_ref[:, 0:LANES]) + br_ref[...]
    ids, wts = _route(logits.T)
    ri_ref[...] = ids
    rw_ref[...] = wts


def _outproj_router(h, acts, ws, g, wr, br):
    t, d = h.shape
    nact = len(acts)
    row_spec = pl.BlockSpec((TM, d), lambda i: (i, 0))
    in_specs = [row_spec]
    in_specs += [pl.BlockSpec((TM, a.shape[1]), lambda i: (i, 0)) for a in acts]
    in_specs += [pl.BlockSpec(w.shape, lambda i: (0, 0)) for w in ws]
    in_specs += [
        pl.BlockSpec((1, d), lambda i: (0, 0)),
        pl.BlockSpec((d, 2 * LANES), lambda i: (0, 0)),
        pl.BlockSpec((1, LANES), lambda i: (0, 0)),
    ]
    return pl.pallas_call(
        functools.partial(_outproj_router_body, nact=nact),
        grid=(t // TM,),
        in_specs=in_specs,
        out_specs=[
            row_spec,
            pl.BlockSpec((TM * _slab_rows(d), LANES), lambda i: (i, 0)),
            pl.BlockSpec((SUBLANES, TM), lambda i: (0, i)),
            pl.BlockSpec((SUBLANES, TM), lambda i: (0, i)),
        ],
        out_shape=[
            jax.ShapeDtypeStruct((t, d), F32),
            jax.ShapeDtypeStruct((t * _slab_rows(d), LANES), F32),
            jax.ShapeDtypeStruct((SUBLANES, t), I32),
            jax.ShapeDtypeStruct((SUBLANES, t), F32),
        ],
        compiler_params=_cparams(("parallel",)),
        name="outproj_router",
    )(h, *acts, *ws, g, wr, br)


def _moe_body(te_ref, nt_ref, tok0_ref, tok_ref, dst_ref, xn_hbm, wup_ref, wdn_ref, y_hbm,
              tok_s, dst_s, xbuf, ybuf, sem_i, sem_g, sem_s):
    i = pl.program_id(0)
    nt = nt_ref[0]
    tm = tok_ref.shape[2]
    d, ff2 = wup_ref.shape
    ff = ff2 // 2
    sr = _slab_rows(d)
    slot = i % 2
    other = 1 - slot
    n_pieces = 2 * MOE_COL_CHUNKS
    rows_per_piece = tm // n_pieces

    def gather(r, s):
        return pltpu.make_async_copy(xn_hbm.at[pl.ds(tok_s[0, 0, r], sr), :], xbuf.at[s, pl.ds(r * sr, sr), :],
                                     sem_g.at[0])

    def scatter(r, s):
        return pltpu.make_async_copy(ybuf.at[s, pl.ds(r * sr, sr), :], y_hbm.at[pl.ds(dst_s[0, 0, r], sr), :],
                                     sem_s.at[0])

    def wait_all(copy, s):
        def body(r, c):
            copy(r, s).wait()
            return c
        lax.fori_loop(0, tm, body, 0, unroll=8)

    @pl.when(i == 0)
    def _():
        ybuf[...] = jnp.zeros_like(ybuf)
        pad = pltpu.make_async_copy(ybuf.at[0], y_hbm.at[pl.ds(y_hbm.shape[0] - tm * sr, tm * sr), :], sem_s.at[0])
        pad.start()
        pad.wait()
        c0 = pltpu.make_async_copy(tok0_ref, tok_s, sem_i.at[0])
        c0.start()
        c0.wait()

        def body(r, c):
            gather(r, 0).start()
            return c
        lax.fori_loop(0, tm, body, 0, unroll=8)

    @pl.when(i <= nt + 1)
    def _():
        wait_all(gather, slot)

        @pl.when(i >= 1)
        def _():
            wait_all(scatter, slot)

        ci = pltpu.make_async_copy(tok_ref, tok_s, sem_i.at[0])
        cd = pltpu.make_async_copy(dst_ref, dst_s, sem_i.at[1])
        ci.start()
        cd.start()
        ci.wait()
        cd.wait()

        def issue(piece):
            for r in range(piece * rows_per_piece, (piece + 1) * rows_per_piece):
                gather(r, other).start(priority=r % 2)
                scatter(r, other).start(priority=r % 2)

        x = jnp.concatenate([p.astype(BF16) for p in _load_slabs(xbuf, tm, d, lead=(slot,))], axis=1)
        cw = ff2 // MOE_COL_CHUNKS
        parts = []
        for c in range(MOE_COL_CHUNKS):
            issue(c)
            parts.append(_dot(x, wup_ref[:, c * cw:(c + 1) * cw].astype(BF16)))
        hcat = jnp.concatenate(parts, axis=1)
        hh = (jax.nn.silu(hcat[:, 0:ff]) * hcat[:, ff:ff2]).astype(BF16)
        cw = d // MOE_COL_CHUNKS
        parts = []
        for c in range(MOE_COL_CHUNKS):
            issue(MOE_COL_CHUNKS + c)
            parts.append(_dot(hh, wdn_ref[:, c * cw:(c + 1) * cw].astype(BF16)))
        _store_slabs(ybuf, jnp.concatenate(parts, axis=1), lead=(slot,))

        @pl.when(i == nt + 1)
        def _():
            wait_all(scatter, other)
            wait_all(gather, other)


def _moe(xn_slabs, tile_expert, num_tiles, tok, dst, w_up, w_down, layer, n_rows_out):
    d, ff2 = w_up.shape[2], w_up.shape[3]
    nrows, _, tm = tok.shape
    sr = _slab_rows(d)
    steps = tile_expert.shape[0]
    assert nrows == steps + 2 and tm % (2 * MOE_COL_CHUNKS) == 0

    def tile_spec(offset):
        return pl.BlockSpec((1, 1, tm), lambda i, te, nt: (i + offset, 0, 0))

    grid_spec = pltpu.PrefetchScalarGridSpec(
        num_scalar_prefetch=2,
        grid=(steps,),
        in_specs=[
            pl.BlockSpec((1, 1, tm), lambda i, te, nt: (1, 0, 0)),
            tile_spec(2),
            tile_spec(0),
            pl.BlockSpec(memory_space=pl.ANY),
            pl.BlockSpec((None, None, d, ff2), lambda i, te, nt: (layer, te[i], 0, 0)),
            pl.BlockSpec((None, None, ff2 // 2, d), lambda i, te, nt: (layer, te[i], 0, 0)),
        ],
        out_specs=pl.BlockSpec(memory_space=pl.ANY),
        scratch_shapes=[
            pltpu.SMEM((1, 1, tm), I32),
            pltpu.SMEM((1, 1, tm), I32),
            pltpu.VMEM((2, tm * sr, LANES), F32),
            pltpu.VMEM((2, tm * sr, LANES), F32),
            pltpu.SemaphoreType.DMA((2,)),
            pltpu.SemaphoreType.DMA((1,)),
            pltpu.SemaphoreType.DMA((1,)),
        ],
    )
    return pl.pallas_call(
        _moe_body,
        grid_spec=grid_spec,
        out_shape=jax.ShapeDtypeStruct((n_rows_out * sr, LANES), F32),
        compiler_params=_cparams(("arbitrary",)),
        name="moe_experts",
    )(tile_expert, num_tiles, tok, tok, dst, xn_slabs, w_up, w_down)


def _dispatch(ri, t, sr):
    tm = TM_MOE
    n_assign = TOP_K * t
    ntiles = n_assign // tm + N_EXPERTS
    e_flat = ri[0:TOP_K].reshape(n_assign)
    e_sorted, a_sorted = lax.sort((e_flat, jnp.arange(n_assign, dtype=I32)), num_keys=1)
    bounds = jnp.sum((e_sorted[None, :] < jnp.arange(N_EXPERTS + 1, dtype=I32)[:, None]).astype(I32), axis=1)
    start = bounds[:-1]
    counts = bounds[1:] - start
    tiles_per = (counts + tm - 1) // tm
    tile_end = jnp.cumsum(tiles_per)
    num_tiles = tile_end[-1:].astype(I32)
    tile_ids = jnp.arange(ntiles, dtype=I32)
    tile_expert = jnp.minimum(jnp.sum((tile_ids[:, None] >= tile_end[None, :]).astype(I32), axis=1), N_EXPERTS - 1)
    last_valid = jnp.take(tile_expert, jnp.maximum(num_tiles[0] - 1, 0))
    tile_expert = jnp.where(tile_ids < num_tiles[0], tile_expert, last_valid).astype(I32)
    rows_before = (tile_ids - jnp.take(tile_end - tiles_per, tile_expert)) * tm
    first = jnp.take(start, tile_expert) + rows_before
    n_valid = jnp.where(tile_ids < num_tiles[0], jnp.clip(jnp.take(counts, tile_expert) - rows_before, 0, tm), 0)
    r = jnp.arange(tm, dtype=I32)
    valid = r[None, :] < n_valid[:, None]
    idx = jnp.clip(first[:, None] + r[None, :], 0, n_assign - 1)
    a = jnp.take(a_sorted, idx)
    tok = (jnp.where(valid, a % t, 0) * sr).astype(I32).reshape(ntiles, 1, tm)
    dst = (jnp.where(valid, a, n_assign + r[None, :]) * sr).astype(I32).reshape(ntiles, 1, tm)
    pad_tok = jnp.zeros((1, 1, tm), I32)
    pad_dst = ((n_assign + jnp.arange(tm, dtype=I32)) * sr).reshape(1, 1, tm)
    tok = jnp.concatenate([pad_tok, tok, pad_tok, pad_tok, pad_tok], axis=0)
    dst = jnp.concatenate([pad_dst, dst, pad_dst, pad_dst, pad_dst], axis=0)
    tile_expert = jnp.concatenate([tile_expert, jnp.broadcast_to(last_valid, (2,)).astype(I32)])
    return tile_expert, num_tiles, tok, dst


def _combine(h_ref, y0_ref, y1_ref, rw_ref):
    m, d = h_ref.shape
    rw = rw_ref[...]
    gates = jnp.concatenate([rw, jnp.zeros((LANES - SUBLANES, m), F32)], axis=0).T
    y0 = jnp.concatenate(_load_slabs(y0_ref, m, d), axis=1)
    y1 = jnp.concatenate(_load_slabs(y1_ref, m, d), axis=1)
    return h_ref[...] + gates[:, 0:1] * y0 + gates[:, 1:2] * y1


def _slab_specs(d, nb):
    rows = TM * _slab_rows(d)
    return [pl.BlockSpec((rows, LANES), lambda i: (i, 0)), pl.BlockSpec((rows, LANES), lambda i: (i + nb, 0)),
            pl.BlockSpec((SUBLANES, TM), lambda i: (0, i))]


def _inproj1_body(h_ref, y0_ref, y1_ref, rw_ref, g_ref, w_ref, hn_ref, qkv_ref):
    h = _combine(h_ref, y0_ref, y1_ref, rw_ref)
    d = h.shape[1]
    hn_ref[...] = h
    xb = _rms(h, g_ref[...]).astype(BF16)
    for c in range(w_ref.shape[1] // d):
        qkv_ref[:, c * d:(c + 1) * d] = _dot(xb, w_ref[:, c * d:(c + 1) * d]).astype(BF16)


def _inproj1(h, y2, rw, g, w):
    t, d = h.shape
    n = w.shape[1]
    nb = t // TM
    row_spec = pl.BlockSpec((TM, d), lambda i: (i, 0))
    return pl.pallas_call(
        _inproj1_body,
        grid=(nb,),
        in_specs=[row_spec] + _slab_specs(d, nb) + [
            pl.BlockSpec((1, d), lambda i: (0, 0)),
            pl.BlockSpec((d, n), lambda i: (0, 0)),
        ],
        out_specs=[row_spec, pl.BlockSpec((TM, n), lambda i: (i, 0))],
        out_shape=[jax.ShapeDtypeStruct((t, d), F32), jax.ShapeDtypeStruct((t, n), BF16)],
        compiler_params=_cparams(("parallel",)),
        name="inproj1",
    )(h, y2, y2, rw, g, w)


def _attn_body(q_ref, k_ref, v_ref, lq1_ref, lk1_ref, lq2_ref, lk2_ref, ng_ref, o_ref,
               bias_s, s_s, p_s, m_s, *, heads, lambda_init):
    h = pl.program_id(1)
    seq, dv = q_ref.shape
    nq = seq // TQ
    dqk = dv // 2
    slope = jnp.exp2(jnp.full((1, 1), -8.0 / heads, F32) * (h + 1).astype(F32))
    scale = dqk ** -0.5

    rowp = lax.broadcasted_iota(I32, (2 * TQ, TK), 0) % TQ
    colp = lax.broadcasted_iota(I32, (2 * TQ, TK), 1)
    allowed = (colp // CHUNK) <= (rowp // CHUNK)
    rowf = rowp.astype(F32)
    bias_s[...] = jnp.where(allowed, slope * (rowf - jnp.abs(rowf - colp.astype(F32))), NEG_INF)
    kcol = slope * lax.broadcasted_iota(I32, (1, TK), 1).astype(F32)
    lam = (jnp.exp(jnp.sum(lq1_ref[...] * lk1_ref[...], axis=-1, keepdims=True))
           - jnp.exp(jnp.sum(lq2_ref[...] * lk2_ref[...], axis=-1, keepdims=True)) + lambda_init)
    gain = ng_ref[pl.ds(h, 1), :] * (1.0 - lambda_init)
    lane = lax.broadcasted_iota(I32, (TQ, dv), 1)

    for qi in range(nq):
        q = q_ref[qi * TQ:(qi + 1) * TQ, :].astype(F32) * scale
        qs = jnp.concatenate([jnp.where(lane < dqk, q, 0.0), jnp.where(lane >= dqk, q, 0.0)], axis=0).astype(BF16)
        mlane = None
        for j in range(qi + 1):
            s = _dot_nt(qs, k_ref[j * TK:(j + 1) * TK, :])
            if j < qi:
                s = s + (kcol - slope * float((qi - j) * TK))
            else:
                s = s + bias_s[...]
            s_s[:, j * TK:(j + 1) * TK] = s
            for c in range(TK // LANES):
                part = s[:, c * LANES:(c + 1) * LANES]
                mlane = part if mlane is None else jnp.maximum(mlane, part)
        m_s[...] = jnp.broadcast_to(jnp.max(mlane, axis=-1, keepdims=True), m_s.shape)
        psum = None
        for j in range(qi + 1):
            for c in range(TK // LANES):
                lo = j * TK + c * LANES
                p = jnp.exp(s_s[:, lo:lo + LANES] - m_s[...])
                psum = p if psum is None else psum + p
                p_s[:, lo:lo + LANES] = p.astype(BF16)
        den = jnp.sum(psum, axis=-1, keepdims=True)
        kv = (qi + 1) * TK
        acc = _dot(p_s[:, 0:kv], v_ref[0:kv, :])
        o1 = acc[0:TQ, :] / den[0:TQ]
        o2 = acc[TQ:2 * TQ, :] / den[TQ:2 * TQ]
        o_ref[qi * TQ:(qi + 1) * TQ, :] = _rms(o1 - lam * o2, gain).astype(BF16)


def _attn(qkv, lq1, lk1, lq2, lk2, norm_g, batch, seq, heads, dv, lambda_init):
    t = qkv.shape[0]
    dqk = dv // 2
    small = pl.BlockSpec((1, dqk), lambda b, h: (0, 0))
    return pl.pallas_call(
        functools.partial(_attn_body, heads=heads, lambda_init=lambda_init),
        grid=(batch, heads),
        in_specs=[
            pl.BlockSpec((seq, dv), lambda b, h: (b, h)),
            pl.BlockSpec((seq, dv), lambda b, h: (b, heads + h)),
            pl.BlockSpec((seq, dv), lambda b, h: (b, 2 * heads + h)),
            small, small, small, small,
            pl.BlockSpec((heads, dv), lambda b, h: (0, 0)),
        ],
        out_specs=pl.BlockSpec((seq, dv), lambda b, h: (b, h)),
        out_shape=jax.ShapeDtypeStruct((t, heads * dv), BF16),
        scratch_shapes=[
            pltpu.VMEM((2 * TQ, TK), F32),
            pltpu.VMEM((2 * TQ, seq), F32),
            pltpu.VMEM((2 * TQ, seq), BF16),
            pltpu.VMEM((2 * TQ, LANES), F32),
        ],
        compiler_params=_cparams(("parallel", "parallel")),
        name="diff_attn",
    )(qkv, qkv, qkv, lq1, lk1, lq2, lk2, norm_g)


def _final_body(h_ref, y0_ref, y1_ref, rw_ref, g_ref, o_ref):
    o_ref[...] = _rms(_combine(h_ref, y0_ref, y1_ref, rw_ref), g_ref[...])


def _final(h, y2, rw, g):
    t, d = h.shape
    nb = t // TM
    row_spec = pl.BlockSpec((TM, d), lambda i: (i, 0))
    return pl.pallas_call(
        _final_body,
        grid=(nb,),
        in_specs=[row_spec] + _slab_specs(d, nb) + [pl.BlockSpec((1, d), lambda i: (0, 0))],
        out_specs=row_spec,
        out_shape=jax.ShapeDtypeStruct((t, d), F32),
        compiler_params=_cparams(("parallel",)),
        name="final_norm",
    )(h, y2, y2, rw, g)


def _block_diag(w):
    nb, bd, _ = w.shape
    eye = jnp.eye(nb, dtype=w.dtype)
    return (eye[:, None, :, None] * w[:, :, None, :]).reshape(nb * bd, nb * bd)


def _router_weights(w_group, b_group, w_expert, b_expert):
    d = w_group.shape[0]
    wr = jnp.zeros((d, LANES), F32)
    wr = wr.at[:, 0:N_GROUPS].set(w_group).at[:, SUBLANES:SUBLANES + N_EXPERTS].set(w_expert)
    br = jnp.zeros((1, LANES), F32)
    br = br.at[0, 0:N_GROUPS].set(b_group).at[0, SUBLANES:SUBLANES + N_EXPERTS].set(b_expert)
    wr_hi = wr.astype(BF16)
    wr_lo = (wr - wr_hi.astype(F32)).astype(BF16)
    return jnp.concatenate([wr_hi, wr_lo], axis=1), br


def _moe_layer(xn_slabs, ri, w_up, w_down, layer):
    t = ri.shape[1]
    tile_expert, num_tiles, tok, dst = _dispatch(ri, t, _slab_rows(w_up.shape[2]))
    return _moe(xn_slabs, tile_expert, num_tiles, tok, dst, w_up, w_down, layer, TOP_K * t + TM_MOE)


def kernel(x, norm_mix_g, norm_ffn_g, ab_w_in, ab_w_out, lru_conv_w, lru_conv_b, lru_w_r, lru_b_r, lru_w_i,
           lru_b_i, lru_lambda, mlstm_b_i, mlstm_b_f, mlstm_norm_g, diff_w_qkv, diff_w_o, diff_lambda_q1,
           diff_lambda_k1, diff_lambda_q2, diff_lambda_k2, diff_norm_g, moe_w_group, moe_b_group, moe_w_expert,
           moe_b_expert, moe_w_up, moe_w_down, final_norm_g):
    batch, seq, d = x.shape
    t = batch * seq
    lru_width = lru_lambda.shape[1]
    m_heads, m_dh = mlstm_norm_g.shape[1], mlstm_norm_g.shape[2]
    m_width = m_heads * m_dh
    a_heads, a_dv = diff_norm_g.shape[1], diff_norm_g.shape[2]
    n_main = 2 * lru_width + 4 * m_width
    assert n_main + 2 * m_heads == ab_w_in.shape[2]
    assert seq % TS_LRU == 0 and seq % TQ == 0 and t % TM == 0 and (TOP_K * t) % TM_MOE == 0
    assert 2 * m_heads <= SUBLANES and TQ == TK and TQ % CHUNK == 0

    x2 = x.reshape(t, d)

    w_in = ab_w_in[0]
    w_gate = jnp.zeros((d, LANES), F32).at[:, 0:2 * m_heads].set(w_in[:, n_main:]).astype(BF16)
    b_gate = jnp.zeros((1, LANES), F32).at[0, 0:m_heads].set(mlstm_b_i[0]).at[0, m_heads:2 * m_heads].set(mlstm_b_f[0])
    pf, qkvo, gate = _inproj0(x2, norm_mix_g[0:1], w_in[:, 0:n_main].astype(BF16), w_gate, b_gate)

    w_ri = jnp.concatenate([_block_diag(lru_w_r[0]), _block_diag(lru_w_i[0])], axis=1).astype(BF16)
    b_ri = jnp.concatenate([lru_b_r[0], lru_b_i[0]])[None, :]
    ya = _lru(pf, lru_conv_w[0], lru_conv_b[0][None, :], w_ri, b_ri, lru_lambda[0][None, :], batch, seq)

    grow = gate[:, 0:SUBLANES].reshape(t // CHUNK, CHUNK, SUBLANES).transpose(0, 2, 1)
    yb = _mlstm(qkvo, gate, grow, mlstm_norm_g[0].reshape(1, m_width), batch, seq, m_heads, m_dh)

    w_out = ab_w_out[0].astype(BF16)
    wr, br = _router_weights(moe_w_group[0], moe_b_group[0], moe_w_expert[0], moe_b_expert[0])
    h1, xn, ri, rw = _outproj_router(x2, [ya, yb], [w_out[0:lru_width], w_out[lru_width:]], norm_ffn_g[0:1], wr, br)
    y2 = _moe_layer(xn, ri, moe_w_up, moe_w_down, 0)

    lambda_init = 0.8 - 0.6 * math.exp(-0.3 * 1)
    h2, qkv = _inproj1(h1, y2, rw, norm_mix_g[1:2], diff_w_qkv[0].astype(BF16))
    attn = _attn(qkv, diff_lambda_q1[0][None, :], diff_lambda_k1[0][None, :], diff_lambda_q2[0][None, :],
                 diff_lambda_k2[0][None, :], diff_norm_g[0], batch, seq, a_heads, a_dv, lambda_init)
    wr, br = _router_weights(moe_w_group[1], moe_b_group[1], moe_w_expert[1], moe_b_expert[1])
    h3, xn, ri, rw = _outproj_router(h2, [attn], [diff_w_o[0].astype(BF16)], norm_ffn_g[1:2], wr, br)
    y2 = _moe_layer(xn, ri, moe_w_up, moe_w_down, 1)

    return _final(h3, y2, rw, final_norm_g[None, :]).reshape(batch, seq, d)
```

```python
import functools
import math

import jax
import jax.numpy as jnp
from jax import lax
from jax.experimental import pallas as pl
from jax.experimental.pallas import tpu as pltpu

F32 = jnp.float32
BF16 = jnp.bfloat16
I32 = jnp.int32

RMS_EPS = 1e-6
CHUNK = 64
LRU_C = 8.0
N_GROUPS = 4
EXPERTS_PER_GROUP = 8
N_EXPERTS = N_GROUPS * EXPERTS_PER_GROUP
TOP_K = 2

LANES = 128
SUBLANES = 8
VMEM_LIMIT = 56 * 1024 * 1024

TM = 512
TS_LRU = 512
TQ = 256
TK = 256
TM_MOE = 512
MLSTM_SEQS = 2
MOE_COL_CHUNKS = 4

NEG_INF = float("-inf")


def _cparams(sem):
    return pltpu.CompilerParams(dimension_semantics=sem, vmem_limit_bytes=VMEM_LIMIT)


def _dot(a, b, precision=None):
    return jnp.dot(a, b, preferred_element_type=F32, precision=precision)


def _dot_nt(a, b):
    return lax.dot_general(a, b, (((1,), (1,)), ((), ())), preferred_element_type=F32)


def _rms(x, g):
    ms = jnp.mean(x * x, axis=-1, keepdims=True)
    return x * lax.rsqrt(ms + RMS_EPS) * g


def _log_sigmoid(x):
    return jnp.minimum(x, 0.0) - jnp.log1p(jnp.exp(-jnp.abs(x)))


def _sigmoid(x):
    return 1.0 / (1.0 + jnp.exp(-x))


def _slab_rows(d):
    return d // LANES


def _store_slabs(ref, x, lead=()):
    m, d = x.shape
    sr = _slab_rows(d)
    for c in range(sr):
        ref[lead + (pl.ds(c, m, stride=sr), slice(None))] = x[:, c * LANES:(c + 1) * LANES]


def _load_slabs(ref, m, d, lead=()):
    sr = _slab_rows(d)
    return [ref[lead + (pl.ds(c, m, stride=sr), slice(None))] for c in range(sr)]


def _inproj0_body(x_ref, g_ref, w_ref, wg_ref, bg_ref, pf_ref, qkvo_ref, gate_ref):
    d = x_ref.shape[1]
    xb = _rms(x_ref[...], g_ref[...]).astype(BF16)
    pf_ref[...] = _dot(xb, w_ref[:, 0:d])
    for c in range(qkvo_ref.shape[1] // d):
        qkvo_ref[:, c * d:(c + 1) * d] = _dot(xb, w_ref[:, (c + 1) * d:(c + 2) * d]).astype(BF16)
    gate_ref[...] = _dot(xb, wg_ref[...]) + bg_ref[...]


def _inproj0(x2, g, w_main, w_gate, b_gate):
    t, d = x2.shape
    n_main = w_main.shape[1]
    grid = (t // TM,)
    return pl.pallas_call(
        _inproj0_body,
        grid=grid,
        in_specs=[
            pl.BlockSpec((TM, d), lambda i: (i, 0)),
            pl.BlockSpec((1, d), lambda i: (0, 0)),
            pl.BlockSpec((d, n_main), lambda i: (0, 0)),
            pl.BlockSpec((d, LANES), lambda i: (0, 0)),
            pl.BlockSpec((1, LANES), lambda i: (0, 0)),
        ],
        out_specs=[
            pl.BlockSpec((TM, d), lambda i: (i, 0)),
            pl.BlockSpec((TM, n_main - d), lambda i: (i, 0)),
            pl.BlockSpec((TM, LANES), lambda i: (i, 0)),
        ],
        out_shape=[
            jax.ShapeDtypeStruct((t, d), F32),
            jax.ShapeDtypeStruct((t, n_main - d), BF16),
            jax.ShapeDtypeStruct((t, LANES), F32),
        ],
        compiler_params=_cparams(("parallel",)),
        name="inproj0",
    )(x2, g, w_main, w_gate, b_gate)


def _lru_body(pf_ref, cw_ref, cb_ref, wri_ref, bri_ref, lam_ref, ya_ref, xbuf, a_s, u_s, hcar):
    ts = pf_ref.shape[0]
    c = ya_ref.shape[1]
    conv_w = cw_ref.shape[0]
    halo = SUBLANES

    @pl.when(pl.program_id(1) == 0)
    def _():
        xbuf[0:halo, :] = jnp.zeros((halo, c), F32)
        hcar[...] = jnp.zeros_like(hcar)

    xbuf[halo:, :] = pf_ref[:, 0:c]
    xc = cb_ref[...] + cw_ref[0:1, :] * xbuf[pl.ds(halo - conv_w + 1, ts), :]
    for j in range(1, conv_w):
        xc = xc + cw_ref[j:j + 1, :] * xbuf[pl.ds(halo - conv_w + 1 + j, ts), :]
    xbuf[0:halo, :] = xbuf[ts:ts + halo, :]

    ri = _dot(xc.astype(BF16), wri_ref[...]) + bri_ref[...]
    r = _sigmoid(ri[:, 0:c])
    i = _sigmoid(ri[:, c:2 * c])
    log_a = LRU_C * r * _log_sigmoid(lam_ref[...])
    a = jnp.exp(log_a)
    a_s[...] = a
    u_s[...] = jnp.sqrt(-jnp.tanh(log_a) * (a * a + 1.0)) * (i * xc)

    row = lax.broadcasted_iota(I32, (SUBLANES, c), 0)

    def blk(j, carry):
        r0 = pl.multiple_of(j * SUBLANES, SUBLANES)
        a = a_s[pl.ds(r0, SUBLANES), :]
        u = u_s[pl.ds(r0, SUBLANES), :]
        for sh in (1, 2, 4):
            a_sh = jnp.where(row >= sh, pltpu.roll(a, sh, 0), 1.0)
            u_sh = jnp.where(row >= sh, pltpu.roll(u, sh, 0), 0.0)
            u = a * u_sh + u
            a = a * a_sh
        h = a * carry + u
        u_s[pl.ds(r0, SUBLANES), :] = h
        return h[SUBLANES - 1:SUBLANES, :]

    hcar[...] = lax.fori_loop(0, ts // SUBLANES, blk, hcar[...], unroll=2)
    ya_ref[...] = (u_s[...] * jax.nn.gelu(pf_ref[:, c:2 * c])).astype(BF16)


def _lru(pf, conv_w, conv_b, w_ri, b_ri, lam, batch, seq):
    t = pf.shape[0]
    c = lam.shape[1]
    ns = seq // TS_LRU
    return pl.pallas_call(
        _lru_body,
        grid=(batch, ns),
        in_specs=[
            pl.BlockSpec((TS_LRU, 2 * c), lambda b, s: (b * ns + s, 0)),
            pl.BlockSpec(conv_w.shape, lambda b, s: (0, 0)),
            pl.BlockSpec((1, c), lambda b, s: (0, 0)),
            pl.BlockSpec((c, 2 * c), lambda b, s: (0, 0)),
            pl.BlockSpec((1, 2 * c), lambda b, s: (0, 0)),
            pl.BlockSpec((1, c), lambda b, s: (0, 0)),
        ],
        out_specs=pl.BlockSpec((TS_LRU, c), lambda b, s: (b * ns + s, 0)),
        out_shape=jax.ShapeDtypeStruct((t, c), BF16),
        scratch_shapes=[
            pltpu.VMEM((TS_LRU + SUBLANES, c), F32),
            pltpu.VMEM((TS_LRU, c), F32),
            pltpu.VMEM((TS_LRU, c), F32),
            pltpu.VMEM((1, c), F32),
        ],
        compiler_params=_cparams(("parallel", "arbitrary")),
        name="rglru",
    )(pf, conv_w, conv_b, w_ri, b_ri, lam)


def _mlstm_body(qkvo_ref, gate_ref, grow_ref, mg_ref, yb_ref, c_s, m_s, *, heads, dh, nseq):
    seq = qkvo_ref.shape[0] // nseq
    nc = seq // CHUNK
    width = heads * dh
    scale = dh ** -0.5
    hi = lax.Precision.HIGHEST

    c_s[...] = jnp.zeros_like(c_s)
    m_s[...] = jnp.zeros_like(m_s)

    rr = lax.broadcasted_iota(I32, (CHUNK, CHUNK), 0)
    cc = lax.broadcasted_iota(I32, (CHUNK, CHUNK), 1)
    causal = cc <= rr
    tril = causal.astype(F32)
    triu = (rr <= cc).astype(F32)
    ones_col = (lax.broadcasted_iota(I32, (CHUNK, dh), 1) == 0).astype(BF16)

    def step(ci, carry):
        for sq in range(nseq):
            chunk_step(ci, sq)
        return carry

    def chunk_step(ci, sq):
        r0 = pl.multiple_of(sq * seq + ci * CHUNK, CHUNK)
        gcol = gate_ref[pl.ds(r0, CHUNK), :]
        grow = grow_ref[sq * nc + ci]
        bcol = _dot(tril, _log_sigmoid(gcol), precision=hi)
        brow = _dot(_log_sigmoid(grow), triu, precision=hi)
        for h in range(heads):
            st = sq * heads + h
            i_col = gcol[:, h:h + 1]
            b_col = bcol[:, heads + h:heads + h + 1]
            i_row = grow[h:h + 1, :]
            b_row = brow[heads + h:heads + h + 1, :]
            q = qkvo_ref[pl.ds(r0, CHUNK), h * dh:(h + 1) * dh]
            k = qkvo_ref[pl.ds(r0, CHUNK), width + h * dh:width + (h + 1) * dh]
            v = qkvo_ref[pl.ds(r0, CHUNK), 2 * width + h * dh:2 * width + (h + 1) * dh]
            o = qkvo_ref[pl.ds(r0, CHUNK), 3 * width + h * dh:3 * width + (h + 1) * dh]
            m_prev = m_s[st:st + 1, 0:1]
            cst = c_s[st]

            log_d = jnp.where(causal, b_col - b_row + i_row, NEG_INF)
            log_inter = b_col + m_prev
            m_t = jnp.maximum(log_inter, jnp.max(log_d, axis=-1, keepdims=True))
            w_intra = jnp.exp(log_d - m_t)
            w_inter = jnp.exp(log_inter - m_t)
            s = _dot_nt(q, k) * scale * w_intra
            v_aug = jnp.concatenate([v, ones_col], axis=1)
            res = w_inter * _dot(q, cst.astype(BF16)) + _dot(s.astype(BF16), v_aug)
            num = res[:, 0:dh]
            den = res[:, dh:dh + 1]
            hh = num / jnp.maximum(jnp.abs(den), jnp.exp(-m_t))
            y = _rms(hh, mg_ref[:, h * dh:(h + 1) * dh]) * _sigmoid(o.astype(F32))
            yb_ref[pl.ds(r0, CHUNK), h * dh:(h + 1) * dh] = y.astype(BF16)

            g = b_col[CHUNK - 1:CHUNK, :]
            log_w = g - b_col + i_col
            m_new = jnp.maximum(g + m_prev, jnp.max(log_w, axis=0, keepdims=True))
            w_state = jnp.exp(log_w - m_new) * scale
            decay = jnp.exp(g + m_prev - m_new)
            wv = (v_aug.astype(F32) * w_state).astype(BF16)
            upd = _dot(k.astype(F32).T.astype(BF16), wv)
            c_s[st] = decay * cst + upd
            m_s[st:st + 1, 0:1] = m_new

    lax.fori_loop(0, nc, step, 0)


def _mlstm(qkvo, gate, grow, mnorm_g, batch, seq, heads, dh):
    t = qkvo.shape[0]
    width = heads * dh
    nc = seq // CHUNK
    nseq = MLSTM_SEQS if batch % MLSTM_SEQS == 0 else 1
    assert nseq * heads <= SUBLANES
    rows = nseq * seq
    return pl.pallas_call(
        functools.partial(_mlstm_body, heads=heads, dh=dh, nseq=nseq),
        grid=(batch // nseq,),
        in_specs=[
            pl.BlockSpec((rows, 4 * width), lambda b: (b, 0)),
            pl.BlockSpec((rows, LANES), lambda b: (b, 0)),
            pl.BlockSpec((nseq * nc, SUBLANES, CHUNK), lambda b: (b, 0, 0)),
            pl.BlockSpec((1, width), lambda b: (0, 0)),
        ],
        out_specs=pl.BlockSpec((rows, width), lambda b: (b, 0)),
        out_shape=jax.ShapeDtypeStruct((t, width), BF16),
        scratch_shapes=[
            pltpu.VMEM((nseq * heads, dh, 2 * dh), F32),
            pltpu.VMEM((SUBLANES, LANES), F32),
        ],
        compiler_params=_cparams(("parallel",)),
        name="mlstm",
    )(qkvo, gate, grow, mnorm_g)


def _route(logits_t):
    tm = logits_t.shape[1]
    row = lax.broadcasted_iota(I32, (SUBLANES, tm), 0)
    rowf = row.astype(F32)
    none = float(SUBLANES)
    gl = jnp.where(row < N_GROUPS, logits_t[0:SUBLANES], NEG_INF)
    gmax = jnp.max(gl, axis=0, keepdims=True)
    gidx = jnp.min(jnp.where(gl == gmax, rowf, none), axis=0, keepdims=True)
    gweight = 1.0 / jnp.sum(jnp.exp(gl - gmax), axis=0, keepdims=True)
    el = logits_t[SUBLANES:2 * SUBLANES]
    for g in range(1, N_GROUPS):
        el = jnp.where(gidx == float(g), logits_t[(g + 1) * SUBLANES:(g + 2) * SUBLANES], el)
    v1 = jnp.max(el, axis=0, keepdims=True)
    i1 = jnp.min(jnp.where(el == v1, rowf, none), axis=0, keepdims=True)
    el2 = jnp.where(rowf == i1, NEG_INF, el)
    v2 = jnp.max(el2, axis=0, keepdims=True)
    i2 = jnp.min(jnp.where(el2 == v2, rowf, none), axis=0, keepdims=True)
    t = jnp.exp(v2 - v1)
    w1 = gweight / (1.0 + t)
    w2 = gweight * t / (1.0 + t)
    e1 = (gidx * EXPERTS_PER_GROUP + i1).astype(I32)
    e2 = (gidx * EXPERTS_PER_GROUP + i2).astype(I32)
    ids = jnp.where(row == 0, e1, jnp.where(row == 1, e2, 0))
    wts = jnp.where(row == 0, w1, jnp.where(row == 1, w2, 0.0))
    return ids, wts


def _outproj_router_body(*refs, nact):
    h_ref = refs[0]
    act_refs = refs[1:1 + nact]
    w_refs = refs[1 + nact:1 + 2 * nact]
    g_ref, wr_ref, br_ref, hn_ref, xn_ref, ri_ref, rw_ref = refs[1 + 2 * nact:]
    acc = h_ref[...]
    for a_ref, w_ref in zip(act_refs, w_refs):
        acc = acc + _dot(a_ref[...], w_ref[...])
    hn_ref[...] = acc
    xn = _rms(acc, g_ref[...])
    _store_slabs(xn_ref, xn)
    xh = xn.astype(BF16)
    xl = (xn - xh.astype(F32)).astype(BF16)
    part = _dot(xh, wr_ref[...])
    logits = part[:, 0:LANES] + part[:, LANES:2 * LANES] + _dot(xl, wr_ref[:, 0:LANES]) + br_ref[...]
    ids, wts = _route(logits.T)
    ri_ref[...] = ids
    rw_ref[...] = wts


def _outproj_router(h, acts, ws, g, wr, br):
    t, d = h.shape
    nact = len(acts)
    row_spec = pl.BlockSpec((TM, d), lambda i: (i, 0))
    in_specs = [row_spec]
    in_specs += [pl.BlockSpec((TM, a.shape[1]), lambda i: (i, 0)) for a in acts]
    in_specs += [pl.BlockSpec(w.shape, lambda i: (0, 0)) for w in ws]
    in_specs += [
        pl.BlockSpec((1, d), lambda i: (0, 0)),
        pl.BlockSpec((d, 2 * LANES), lambda i: (0, 0)),
        pl.BlockSpec((1, LANES), lambda i: (0, 0)),
    ]
    return pl.pallas_call(
        functools.partial(_outproj_router_body, nact=nact),
        grid=(t // TM,),
        in_specs=in_specs,
        out_specs=[
            row_spec,
            pl.BlockSpec((TM * _slab_rows(d), LANES), lambda i: (i, 0)),
            pl.BlockSpec((SUBLANES, TM), lambda i: (0, i)),
            pl.BlockSpec((SUBLANES, TM), lambda i: (0, i)),
        ],
        out_shape=[
            jax.ShapeDtypeStruct((t, d), F32),
            jax.ShapeDtypeStruct((t * _slab_rows(d), LANES), F32),
            jax.ShapeDtypeStruct((SUBLANES, t), I32),
            jax.ShapeDtypeStruct((SUBLANES, t), F32),
        ],
        compiler_params=_cparams(("parallel",)),
        name="outproj_router",
    )(h, *acts, *ws, g, wr, br)


def _moe_body(te_ref, nt_ref, dst_ref, x_ref, wup_ref, wdn_ref, y_hbm, dst_s, ybuf, sem_i, sem_s):
    i = pl.program_id(0)
    nt = nt_ref[0]
    tm = dst_ref.shape[2]
    d, ff2 = wup_ref.shape
    ff = ff2 // 2
    sr = _slab_rows(d)
    slot = i % 2
    other = 1 - slot
    n_pieces = 2 * MOE_COL_CHUNKS
    rows_per_piece = tm // n_pieces

    def scatter(r, s):
        return pltpu.make_async_copy(ybuf.at[s, pl.ds(r * sr, sr), :], y_hbm.at[pl.ds(dst_s[0, 0, r], sr), :],
                                     sem_s.at[0])

    def wait_all(copy, s):
        def body(r, c):
            copy(r, s).wait()
            return c
        lax.fori_loop(0, tm, body, 0, unroll=8)

    @pl.when(i == 0)
    def _():
        ybuf[...] = jnp.zeros_like(ybuf)
        pad = pltpu.make_async_copy(ybuf.at[0], y_hbm.at[pl.ds(y_hbm.shape[0] - tm * sr, tm * sr), :], sem_s.at[0])
        pad.start()
        pad.wait()

    @pl.when(i <= nt)
    def _():
        @pl.when(i >= 1)
        def _():
            wait_all(scatter, slot)

        cd = pltpu.make_async_copy(dst_ref, dst_s, sem_i.at[0])
        cd.start()
        cd.wait()

        def issue(piece):
            for r in range(piece * rows_per_piece, (piece + 1) * rows_per_piece):
                scatter(r, other).start(priority=r % 2)

        x = jnp.concatenate([p.astype(BF16) for p in _load_slabs(x_ref, tm, d)], axis=1)
        cw = ff2 // MOE_COL_CHUNKS
        parts = []
        for c in range(MOE_COL_CHUNKS):
            issue(c)
            parts.append(_dot(x, wup_ref[:, c * cw:(c + 1) * cw].astype(BF16)))
        hcat = jnp.concatenate(parts, axis=1)
        hh = (jax.nn.silu(hcat[:, 0:ff]) * hcat[:, ff:ff2]).astype(BF16)
        cw = d // MOE_COL_CHUNKS
        parts = []
        for c in range(MOE_COL_CHUNKS):
            issue(MOE_COL_CHUNKS + c)
            parts.append(_dot(hh, wdn_ref[:, c * cw:(c + 1) * cw].astype(BF16)))
        _store_slabs(ybuf, jnp.concatenate(parts, axis=1), lead=(slot,))

        @pl.when(i == nt)
        def _():
            wait_all(scatter, other)


def _moe(xs_slabs, tile_expert, num_tiles, dst, w_up, w_down, layer, n_rows_out):
    d, ff2 = w_up.shape[2], w_up.shape[3]
    nrows, _, tm = dst.shape
    sr = _slab_rows(d)
    steps = tile_expert.shape[0]
    assert nrows == steps and tm % (2 * MOE_COL_CHUNKS) == 0

    grid_spec = pltpu.PrefetchScalarGridSpec(
        num_scalar_prefetch=2,
        grid=(steps,),
        in_specs=[
            pl.BlockSpec((1, 1, tm), lambda i, te, nt: (i, 0, 0)),
            pl.BlockSpec((tm * sr, LANES), lambda i, te, nt: (jnp.minimum(i, nt[0] - 1), 0)),
            pl.BlockSpec((None, None, d, ff2), lambda i, te, nt: (layer, te[i], 0, 0)),
            pl.BlockSpec((None, None, ff2 // 2, d), lambda i, te, nt: (layer, te[i], 0, 0)),
        ],
        out_specs=pl.BlockSpec(memory_space=pl.ANY),
        scratch_shapes=[
            pltpu.SMEM((1, 1, tm), I32),
            pltpu.VMEM((2, tm * sr, LANES), F32),
            pltpu.SemaphoreType.DMA((1,)),
            pltpu.SemaphoreType.DMA((1,)),
        ],
    )
    return pl.pallas_call(
        _moe_body,
        grid_spec=grid_spec,
        out_shape=jax.ShapeDtypeStruct((n_rows_out * sr, LANES), F32),
        compiler_params=_cparams(("arbitrary",)),
        name="moe_experts",
    )(tile_expert, num_tiles, dst, xs_slabs, w_up, w_down)


def _scatter_rows_body(pos_ref, x_ref, xs_in, xs_hbm, pos_s, sem_i, sem_s):
    del xs_in
    m = pos_ref.shape[1]
    sr = x_ref.shape[0] // m

    cp = pltpu.make_async_copy(pos_ref, pos_s, sem_i.at[0])
    cp.start()
    cp.wait()

    def put(k, r):
        return pltpu.make_async_copy(x_ref.at[pl.ds(r * sr, sr), :], xs_hbm.at[pl.ds(pos_s[k, r], sr), :], sem_s.at[0])

    for r in range(m):
        for k in range(TOP_K):
            put(k, r).start(priority=r % 2)

    def drain(r, c):
        for k in range(TOP_K):
            put(k, r).wait()
        return c
    lax.fori_loop(0, m, drain, 0, unroll=8)


def _scatter_rows(xn_slabs, pos, n_rows_out):
    t = pos.shape[1]
    sr = xn_slabs.shape[0] // t
    return pl.pallas_call(
        _scatter_rows_body,
        grid=(t // TM,),
        in_specs=[
            pl.BlockSpec((TOP_K, TM), lambda i: (0, i)),
            pl.BlockSpec((TM * sr, LANES), lambda i: (i, 0)),
            pl.BlockSpec(memory_space=pl.ANY),
        ],
        out_specs=pl.BlockSpec(memory_space=pl.ANY),
        out_shape=jax.ShapeDtypeStruct((n_rows_out * sr, LANES), F32),
        scratch_shapes=[
            pltpu.SMEM((TOP_K, TM), I32),
            pltpu.SemaphoreType.DMA((1,)),
            pltpu.SemaphoreType.DMA((1,)),
        ],
        input_output_aliases={2: 0},
        compiler_params=_cparams(("arbitrary",)),
        name="moe_dispatch",
    )(pos, xn_slabs, jnp.zeros((n_rows_out * sr, LANES), F32))


def _dispatch(ri, t, sr):
    tm = TM_MOE
    n_assign = TOP_K * t
    ntiles = n_assign // tm + N_EXPERTS
    e_flat = ri[0:TOP_K].reshape(n_assign)
    e_sorted, a_sorted = lax.sort((e_flat, jnp.arange(n_assign, dtype=I32)), num_keys=1)
    bounds = jnp.sum((e_sorted[None, :] < jnp.arange(N_EXPERTS + 1, dtype=I32)[:, None]).astype(I32), axis=1)
    start = bounds[:-1]
    counts = bounds[1:] - start
    tiles_per = (counts + tm - 1) // tm
    tile_end = jnp.cumsum(tiles_per)
    num_tiles = tile_end[-1:].astype(I32)
    tile_ids = jnp.arange(ntiles, dtype=I32)
    tile_expert = jnp.minimum(jnp.sum((tile_ids[:, None] >= tile_end[None, :]).astype(I32), axis=1), N_EXPERTS - 1)
    last_valid = jnp.take(tile_expert, jnp.maximum(num_tiles[0] - 1, 0))
    tile_expert = jnp.where(tile_ids < num_tiles[0], tile_expert, last_valid).astype(I32)
    rows_before = (tile_ids - jnp.take(tile_end - tiles_per, tile_expert)) * tm
    first = jnp.take(start, tile_expert) + rows_before
    n_valid = jnp.where(tile_ids < num_tiles[0], jnp.clip(jnp.take(counts, tile_expert) - rows_before, 0, tm), 0)
    r = jnp.arange(tm, dtype=I32)
    valid = r[None, :] < n_valid[:, None]
    idx = jnp.clip(first[:, None] + r[None, :], 0, n_assign - 1)
    a = jnp.take(a_sorted, idx)
    dst = (jnp.where(valid, a, n_assign + r[None, :]) * sr).astype(I32).reshape(ntiles, 1, tm)
    pad_dst = ((n_assign + jnp.arange(tm, dtype=I32)) * sr).reshape(1, 1, tm)
    dst = jnp.concatenate([pad_dst, dst], axis=0)
    tile_expert = jnp.concatenate([tile_expert, jnp.broadcast_to(last_valid, (1,)).astype(I32)])
    shift = (tile_end - tiles_per) * tm - start
    onehot = e_sorted[None, :] == jnp.arange(N_EXPERTS, dtype=I32)[:, None]
    pos_sorted = jnp.arange(n_assign, dtype=I32) + jnp.sum(jnp.where(onehot, shift[:, None], 0), axis=0)
    _, pos = lax.sort((a_sorted, pos_sorted), num_keys=1)
    pos = (pos * sr).astype(I32).reshape(TOP_K, t)
    return tile_expert, num_tiles, dst, pos


def _combine(h_ref, y0_ref, y1_ref, rw_ref):
    m, d = h_ref.shape
    rw = rw_ref[...]
    gates = jnp.concatenate([rw, jnp.zeros((LANES - SUBLANES, m), F32)], axis=0).T
    y0 = jnp.concatenate(_load_slabs(y0_ref, m, d), axis=1)
    y1 = jnp.concatenate(_load_slabs(y1_ref, m, d), axis=1)
    return h_ref[...] + gates[:, 0:1] * y0 + gates[:, 1:2] * y1


def _slab_specs(d, nb):
    rows = TM * _slab_rows(d)
    return [pl.BlockSpec((rows, LANES), lambda i: (i, 0)), pl.BlockSpec((rows, LANES), lambda i: (i + nb, 0)),
            pl.BlockSpec((SUBLANES, TM), lambda i: (0, i))]


def _inproj1_body(h_ref, y0_ref, y1_ref, rw_ref, g_ref, w_ref, hn_ref, qkv_ref):
    h = _combine(h_ref, y0_ref, y1_ref, rw_ref)
    d = h.shape[1]
    hn_ref[...] = h
    xb = _rms(h, g_ref[...]).astype(BF16)
    for c in range(w_ref.shape[1] // d):
        qkv_ref[:, c * d:(c + 1) * d] = _dot(xb, w_ref[:, c * d:(c + 1) * d]).astype(BF16)


def _inproj1(h, y2, rw, g, w):
    t, d = h.shape
    n = w.shape[1]
    nb = t // TM
    row_spec = pl.BlockSpec((TM, d), lambda i: (i, 0))
    return pl.pallas_call(
        _inproj1_body,
        grid=(nb,),
        in_specs=[row_spec] + _slab_specs(d, nb) + [
            pl.BlockSpec((1, d), lambda i: (0, 0)),
            pl.BlockSpec((d, n), lambda i: (0, 0)),
        ],
        out_specs=[row_spec, pl.BlockSpec((TM, n), lambda i: (i, 0))],
        out_shape=[jax.ShapeDtypeStruct((t, d), F32), jax.ShapeDtypeStruct((t, n), BF16)],
        compiler_params=_cparams(("parallel",)),
        name="inproj1",
    )(h, y2, y2, rw, g, w)


def _attn_body(q_ref, k_ref, v_ref, lq1_ref, lk1_ref, lq2_ref, lk2_ref, ng_ref, o_ref,
               vaug_s, bias_s, s_s, p_s, m_s, *, heads, lambda_init):
    h = pl.program_id(1)
    seq, dv = q_ref.shape
    nq = seq // TQ
    dqk = dv // 2
    slope = jnp.exp2(jnp.full((1, 1), -8.0 / heads, F32) * (h + 1).astype(F32))
    scale = dqk ** -0.5

    vaug_s[:, 0:dv] = v_ref[...]
    vaug_s[:, dv:2 * dv] = (lax.broadcasted_iota(I32, (seq, dv), 1) == 0).astype(BF16)

    rowp = lax.broadcasted_iota(I32, (2 * TQ, TK), 0) % TQ
    colp = lax.broadcasted_iota(I32, (2 * TQ, TK), 1)
    allowed = (colp // CHUNK) <= (rowp // CHUNK)
    rowf = rowp.astype(F32)
    bias_s[...] = jnp.where(allowed, slope * (rowf - jnp.abs(rowf - colp.astype(F32))), NEG_INF)
    kcol = slope * lax.broadcasted_iota(I32, (1, TK), 1).astype(F32)
    lam = (jnp.exp(jnp.sum(lq1_ref[...] * lk1_ref[...], axis=-1, keepdims=True))
           - jnp.exp(jnp.sum(lq2_ref[...] * lk2_ref[...], axis=-1, keepdims=True)) + lambda_init)
    gain = ng_ref[pl.ds(h, 1), :] * (1.0 - lambda_init)
    lane = lax.broadcasted_iota(I32, (TQ, dv), 1)

    for qi in range(nq):
        q = q_ref[qi * TQ:(qi + 1) * TQ, :].astype(F32) * scale
        qs = jnp.concatenate([jnp.where(lane < dqk, q, 0.0), jnp.where(lane >= dqk, q, 0.0)], axis=0).astype(BF16)
        mlane = None
        for j in range(qi + 1):
            s = _dot_nt(qs, k_ref[j * TK:(j + 1) * TK, :])
            if j < qi:
                s = s + (kcol - slope * float((qi - j) * TK))
            else:
                s = s + bias_s[...]
            s_s[:, j * TK:(j + 1) * TK] = s
            for c in range(TK // LANES):
                part = s[:, c * LANES:(c + 1) * LANES]
                mlane = part if mlane is None else jnp.maximum(mlane, part)
        m_s[...] = jnp.broadcast_to(jnp.max(mlane, axis=-1, keepdims=True), m_s.shape)
        for j in range(qi + 1):
            for c in range(TK // LANES):
                lo = j * TK + c * LANES
                p_s[:, lo:lo + LANES] = jnp.exp(s_s[:, lo:lo + LANES] - m_s[...]).astype(BF16)
        kv = (qi + 1) * TK
        acc = _dot(p_s[:, 0:kv], vaug_s[0:kv, :])
        o1 = acc[0:TQ, 0:dv] / acc[0:TQ, dv:dv + 1]
        o2 = acc[TQ:2 * TQ, 0:dv] / acc[TQ:2 * TQ, dv:dv + 1]
        o_ref[qi * TQ:(qi + 1) * TQ, :] = _rms(o1 - lam * o2, gain).astype(BF16)


def _attn(qkv, lq1, lk1, lq2, lk2, norm_g, batch, seq, heads, dv, lambda_init):
    t = qkv.shape[0]
    dqk = dv // 2
    small = pl.BlockSpec((1, dqk), lambda b, h: (0, 0))
    return pl.pallas_call(
        functools.partial(_attn_body, heads=heads, lambda_init=lambda_init),
        grid=(batch, heads),
        in_specs=[
            pl.BlockSpec((seq, dv), lambda b, h: (b, h)),
            pl.BlockSpec((seq, dv), lambda b, h: (b, heads + h)),
            pl.BlockSpec((seq, dv), lambda b, h: (b, 2 * heads + h)),
            small, small, small, small,
            pl.BlockSpec((heads, dv), lambda b, h: (0, 0)),
        ],
        out_specs=pl.BlockSpec((seq, dv), lambda b, h: (b, h)),
        out_shape=jax.ShapeDtypeStruct((t, heads * dv), BF16),
        scratch_shapes=[
            pltpu.VMEM((seq, 2 * dv), BF16),
            pltpu.VMEM((2 * TQ, TK), F32),
            pltpu.VMEM((2 * TQ, seq), F32),
            pltpu.VMEM((2 * TQ, seq), BF16),
            pltpu.VMEM((2 * TQ, LANES), F32),
        ],
        compiler_params=_cparams(("parallel", "parallel")),
        name="diff_attn",
    )(qkv, qkv, qkv, lq1, lk1, lq2, lk2, norm_g)


def _final_body(h_ref, y0_ref, y1_ref, rw_ref, g_ref, o_ref):
    o_ref[...] = _rms(_combine(h_ref, y0_ref, y1_ref, rw_ref), g_ref[...])


def _final(h, y2, rw, g):
    t, d = h.shape
    nb = t // TM
    row_spec = pl.BlockSpec((TM, d), lambda i: (i, 0))
    return pl.pallas_call(
        _final_body,
        grid=(nb,),
        in_specs=[row_spec] + _slab_specs(d, nb) + [pl.BlockSpec((1, d), lambda i: (0, 0))],
        out_specs=row_spec,
        out_shape=jax.ShapeDtypeStruct((t, d), F32),
        compiler_params=_cparams(("parallel",)),
        name="final_norm",
    )(h, y2, y2, rw, g)


def _block_diag(w):
    nb, bd, _ = w.shape
    eye = jnp.eye(nb, dtype=w.dtype)
    return (eye[:, None, :, None] * w[:, :, None, :]).reshape(nb * bd, nb * bd)


def _router_weights(w_group, b_group, w_expert, b_expert):
    d = w_group.shape[0]
    wr = jnp.zeros((d, LANES), F32)
    wr = wr.at[:, 0:N_GROUPS].set(w_group).at[:, SUBLANES:SUBLANES + N_EXPERTS].set(w_expert)
    br = jnp.zeros((1, LANES), F32)
    br = br.at[0, 0:N_GROUPS].set(b_group).at[0, SUBLANES:SUBLANES + N_EXPERTS].set(b_expert)
    wr_hi = wr.astype(BF16)
    wr_lo = (wr - wr_hi.astype(F32)).astype(BF16)
    return jnp.concatenate([wr_hi, wr_lo], axis=1), br


def _moe_layer(xn_slabs, ri, w_up, w_down, layer):
    t = ri.shape[1]
    tile_expert, num_tiles, dst, pos = _dispatch(ri, t, _slab_rows(w_up.shape[2]))
    xs_slabs = _scatter_rows(xn_slabs, pos, (tile_expert.shape[0] - 1) * TM_MOE)
    return _moe(xs_slabs, tile_expert, num_tiles, dst, w_up, w_down, layer, TOP_K * t + TM_MOE)


def kernel(x, norm_mix_g, norm_ffn_g, ab_w_in, ab_w_out, lru_conv_w, lru_conv_b, lru_w_r, lru_b_r, lru_w_i,
           lru_b_i, lru_lambda, mlstm_b_i, mlstm_b_f, mlstm_norm_g, diff_w_qkv, diff_w_o, diff_lambda_q1,
           diff_lambda_k1, diff_lambda_q2, diff_lambda_k2, diff_norm_g, moe_w_group, moe_b_group, moe_w_expert,
           moe_b_expert, moe_w_up, moe_w_down, final_norm_g):
    batch, seq, d = x.shape
    t = batch * seq
    lru_width = lru_lambda.shape[1]
    m_heads, m_dh = mlstm_norm_g.shape[1], mlstm_norm_g.shape[2]
    m_width = m_heads * m_dh
    a_heads, a_dv = diff_norm_g.shape[1], diff_norm_g.shape[2]
    n_main = 2 * lru_width + 4 * m_width
    assert n_main + 2 * m_heads == ab_w_in.shape[2]
    assert seq % TS_LRU == 0 and seq % TQ == 0 and t % TM == 0 and (TOP_K * t) % TM_MOE == 0
    assert 2 * m_heads <= SUBLANES and TQ == TK and TQ % CHUNK == 0

    x2 = x.reshape(t, d)

    w_in = ab_w_in[0]
    w_gate = jnp.zeros((d, LANES), F32).at[:, 0:2 * m_heads].set(w_in[:, n_main:]).astype(BF16)
    b_gate = jnp.zeros((1, LANES), F32).at[0, 0:m_heads].set(mlstm_b_i[0]).at[0, m_heads:2 * m_heads].set(mlstm_b_f[0])
    pf, qkvo, gate = _inproj0(x2, norm_mix_g[0:1], w_in[:, 0:n_main].astype(BF16), w_gate, b_gate)

    w_ri = jnp.concatenate([_block_diag(lru_w_r[0]), _block_diag(lru_w_i[0])], axis=1).astype(BF16)
    b_ri = jnp.concatenate([lru_b_r[0], lru_b_i[0]])[None, :]
    ya = _lru(pf, lru_conv_w[0], lru_conv_b[0][None, :], w_ri, b_ri, lru_lambda[0][None, :], batch, seq)

    grow = gate[:, 0:SUBLANES].reshape(t // CHUNK, CHUNK, SUBLANES).transpose(0, 2, 1)
    yb = _mlstm(qkvo, gate, grow, mlstm_norm_g[0].reshape(1, m_width), batch, seq, m_heads, m_dh)

    w_out = ab_w_out[0].astype(BF16)
    wr, br = _router_weights(moe_w_group[0], moe_b_group[0], moe_w_expert[0], moe_b_expert[0])
    h1, xn, ri, rw = _outproj_router(x2, [ya, yb], [w_out[0:lru_width], w_out[lru_width:]], norm_ffn_g[0:1], wr, br)
    y2 = _moe_layer(xn, ri, moe_w_up, moe_w_down, 0)

    lambda_init = 0.8 - 0.6 * math.exp(-0.3 * 1)
    h2, qkv = _inproj1(h1, y2, rw, norm_mix_g[1:2], diff_w_qkv[0].astype(BF16))
    attn = _attn(qkv, diff_lambda_q1[0][None, :], diff_lambda_k1[0][None, :], diff_lambda_q2[0][None, :],
                 diff_lambda_k2[0][None, :], diff_norm_g[0], batch, seq, a_heads, a_dv, lambda_init)
    wr, br = _router_weights(moe_w_group[1], moe_b_group[1], moe_w_expert[1], moe_b_expert[1])
    h3, xn, ri, rw = _outproj_router(h2, [attn], [diff_w_o[0].astype(BF16)], norm_ffn_g[1:2], wr, br)
    y2 = _moe_layer(xn, ri, moe_w_up, moe_w_down, 1)

    return _final(h3, y2, rw, final_norm_g[None, :]).reshape(batch, seq, d)
```

```python
import functools
import math

import jax
import jax.numpy as jnp
from jax import lax
from jax.experimental import pallas as pl
from jax.experimental.pallas import tpu as pltpu

F32 = jnp.float32
BF16 = jnp.bfloat16
I32 = jnp.int32

RMS_EPS = 1e-6
CHUNK = 64
LRU_C = 8.0
N_GROUPS = 4
EXPERTS_PER_GROUP = 8
N_EXPERTS = N_GROUPS * EXPERTS_PER_GROUP
TOP_K = 2

LANES = 128
SUBLANES = 8
VMEM_LIMIT = 56 * 1024 * 1024

TM = 512
TS_LRU = 512
TQ = 256
TK = 256
TM_MOE = 512
MLSTM_SEQS = 2
MOE_COL_CHUNKS = 4

NEG_INF = float("-inf")


def _cparams(sem):
    return pltpu.CompilerParams(dimension_semantics=sem, vmem_limit_bytes=VMEM_LIMIT)


def _dot(a, b, precision=None):
    return jnp.dot(a, b, preferred_element_type=F32, precision=precision)


def _dot_nt(a, b):
    return lax.dot_general(a, b, (((1,), (1,)), ((), ())), preferred_element_type=F32)


def _rms(x, g):
    ms = jnp.mean(x * x, axis=-1, keepdims=True)
    return x * lax.rsqrt(ms + RMS_EPS) * g


def _log_sigmoid(x):
    return jnp.minimum(x, 0.0) - jnp.log1p(jnp.exp(-jnp.abs(x)))


def _sigmoid(x):
    return 1.0 / (1.0 + jnp.exp(-x))


def _slab_rows(d):
    return d // LANES


def _store_slabs(ref, x, lead=()):
    m, d = x.shape
    sr = _slab_rows(d)
    for c in range(sr):
        ref[lead + (pl.ds(c, m, stride=sr), slice(None))] = x[:, c * LANES:(c + 1) * LANES]


def _load_slabs(ref, m, d, lead=()):
    sr = _slab_rows(d)
    return [ref[lead + (pl.ds(c, m, stride=sr), slice(None))] for c in range(sr)]


def _inproj0_body(x_ref, g_ref, w_ref, wg_ref, bg_ref, pf_ref, qkvo_ref, gate_ref):
    d = x_ref.shape[1]
    xb = _rms(x_ref[...], g_ref[...]).astype(BF16)
    pf_ref[...] = _dot(xb, w_ref[:, 0:d])
    for c in range(qkvo_ref.shape[1] // d):
        qkvo_ref[:, c * d:(c + 1) * d] = _dot(xb, w_ref[:, (c + 1) * d:(c + 2) * d]).astype(BF16)
    gate_ref[...] = _dot(xb, wg_ref[...]) + bg_ref[...]


def _inproj0(x2, g, w_main, w_gate, b_gate):
    t, d = x2.shape
    n_main = w_main.shape[1]
    grid = (t // TM,)
    return pl.pallas_call(
        _inproj0_body,
        grid=grid,
        in_specs=[
            pl.BlockSpec((TM, d), lambda i: (i, 0)),
            pl.BlockSpec((1, d), lambda i: (0, 0)),
            pl.BlockSpec((d, n_main), lambda i: (0, 0)),
            pl.BlockSpec((d, LANES), lambda i: (0, 0)),
            pl.BlockSpec((1, LANES), lambda i: (0, 0)),
        ],
        out_specs=[
            pl.BlockSpec((TM, d), lambda i: (i, 0)),
            pl.BlockSpec((TM, n_main - d), lambda i: (i, 0)),
            pl.BlockSpec((TM, LANES), lambda i: (i, 0)),
        ],
        out_shape=[
            jax.ShapeDtypeStruct((t, d), F32),
            jax.ShapeDtypeStruct((t, n_main - d), BF16),
            jax.ShapeDtypeStruct((t, LANES), F32),
        ],
        compiler_params=_cparams(("parallel",)),
        name="inproj0",
    )(x2, g, w_main, w_gate, b_gate)


def _lru_body(pf_ref, cw_ref, cb_ref, wri_ref, bri_ref, lam_ref, ya_ref, xbuf, a_s, u_s, hcar):
    ts = pf_ref.shape[0]
    c = ya_ref.shape[1]
    conv_w = cw_ref.shape[0]
    halo = SUBLANES

    @pl.when(pl.program_id(1) == 0)
    def _():
        xbuf[0:halo, :] = jnp.zeros((halo, c), F32)
        hcar[...] = jnp.zeros_like(hcar)

    xbuf[halo:, :] = pf_ref[:, 0:c]
    xc = cb_ref[...] + cw_ref[0:1, :] * xbuf[pl.ds(halo - conv_w + 1, ts), :]
    for j in range(1, conv_w):
        xc = xc + cw_ref[j:j + 1, :] * xbuf[pl.ds(halo - conv_w + 1 + j, ts), :]
    xbuf[0:halo, :] = xbuf[ts:ts + halo, :]

    ri = _dot(xc.astype(BF16), wri_ref[...]) + bri_ref[...]
    r = _sigmoid(ri[:, 0:c])
    i = _sigmoid(ri[:, c:2 * c])
    log_a = LRU_C * r * _log_sigmoid(lam_ref[...])
    a = jnp.exp(log_a)
    a_s[...] = a
    u_s[...] = jnp.sqrt(-jnp.tanh(log_a) * (a * a + 1.0)) * (i * xc)

    row = lax.broadcasted_iota(I32, (SUBLANES, c), 0)

    def blk(j, carry):
        r0 = pl.multiple_of(j * SUBLANES, SUBLANES)
        a = a_s[pl.ds(r0, SUBLANES), :]
        u = u_s[pl.ds(r0, SUBLANES), :]
        for sh in (1, 2, 4):
            a_sh = jnp.where(row >= sh, pltpu.roll(a, sh, 0), 1.0)
            u_sh = jnp.where(row >= sh, pltpu.roll(u, sh, 0), 0.0)
            u = a * u_sh + u
            a = a * a_sh
        h = a * carry + u
        u_s[pl.ds(r0, SUBLANES), :] = h
        return h[SUBLANES - 1:SUBLANES, :]

    hcar[...] = lax.fori_loop(0, ts // SUBLANES, blk, hcar[...], unroll=2)
    ya_ref[...] = (u_s[...] * jax.nn.gelu(pf_ref[:, c:2 * c])).astype(BF16)


def _lru(pf, conv_w, conv_b, w_ri, b_ri, lam, batch, seq):
    t = pf.shape[0]
    c = lam.shape[1]
    ns = seq // TS_LRU
    return pl.pallas_call(
        _lru_body,
        grid=(batch, ns),
        in_specs=[
            pl.BlockSpec((TS_LRU, 2 * c), lambda b, s: (b * ns + s, 0)),
            pl.BlockSpec(conv_w.shape, lambda b, s: (0, 0)),
            pl.BlockSpec((1, c), lambda b, s: (0, 0)),
            pl.BlockSpec((c, 2 * c), lambda b, s: (0, 0)),
            pl.BlockSpec((1, 2 * c), lambda b, s: (0, 0)),
            pl.BlockSpec((1, c), lambda b, s: (0, 0)),
        ],
        out_specs=pl.BlockSpec((TS_LRU, c), lambda b, s: (b * ns + s, 0)),
        out_shape=jax.ShapeDtypeStruct((t, c), BF16),
        scratch_shapes=[
            pltpu.VMEM((TS_LRU + SUBLANES, c), F32),
            pltpu.VMEM((TS_LRU, c), F32),
            pltpu.VMEM((TS_LRU, c), F32),
            pltpu.VMEM((1, c), F32),
        ],
        compiler_params=_cparams(("parallel", "arbitrary")),
        name="rglru",
    )(pf, conv_w, conv_b, w_ri, b_ri, lam)


def _mlstm_body(qkvo_ref, gate_ref, grow_ref, mg_ref, yb_ref, c_s, m_s, *, heads, dh, nseq):
    seq = qkvo_ref.shape[0] // nseq
    nc = seq // CHUNK
    width = heads * dh
    scale = dh ** -0.5
    hi = lax.Precision.HIGHEST

    c_s[...] = jnp.zeros_like(c_s)
    m_s[...] = jnp.zeros_like(m_s)

    rr = lax.broadcasted_iota(I32, (CHUNK, CHUNK), 0)
    cc = lax.broadcasted_iota(I32, (CHUNK, CHUNK), 1)
    causal = cc <= rr
    tril = causal.astype(F32)
    triu = (rr <= cc).astype(F32)
    ones_col = (lax.broadcasted_iota(I32, (CHUNK, dh), 1) == 0).astype(BF16)

    def step(ci, carry):
        for sq in range(nseq):
            chunk_step(ci, sq)
        return carry

    def chunk_step(ci, sq):
        r0 = pl.multiple_of(sq * seq + ci * CHUNK, CHUNK)
        gcol = gate_ref[pl.ds(r0, CHUNK), :]
        grow = grow_ref[sq * nc + ci]
        bcol = _dot(tril, _log_sigmoid(gcol), precision=hi)
        brow = _dot(_log_sigmoid(grow), triu, precision=hi)
        for h in range(heads):
            st = sq * heads + h
            i_col = gcol[:, h:h + 1]
            b_col = bcol[:, heads + h:heads + h + 1]
            i_row = grow[h:h + 1, :]
            b_row = brow[heads + h:heads + h + 1, :]
            q = qkvo_ref[pl.ds(r0, CHUNK), h * dh:(h + 1) * dh]
            k = qkvo_ref[pl.ds(r0, CHUNK), width + h * dh:width + (h + 1) * dh]
            v = qkvo_ref[pl.ds(r0, CHUNK), 2 * width + h * dh:2 * width + (h + 1) * dh]
            o = qkvo_ref[pl.ds(r0, CHUNK), 3 * width + h * dh:3 * width + (h + 1) * dh]
            m_prev = m_s[st:st + 1, 0:1]
            cst = c_s[st]

            log_d = jnp.where(causal, b_col - b_row + i_row, NEG_INF)
            log_inter = b_col + m_prev
            m_t = jnp.maximum(log_inter, jnp.max(log_d, axis=-1, keepdims=True))
            w_intra = jnp.exp(log_d - m_t)
            w_inter = jnp.exp(log_inter - m_t)
            s = _dot_nt(q, k) * scale * w_intra
            v_aug = jnp.concatenate([v, ones_col], axis=1)
            res = w_inter * _dot(q, cst.astype(BF16)) + _dot(s.astype(BF16), v_aug)
            num = res[:, 0:dh]
            den = res[:, dh:dh + 1]
            hh = num / jnp.maximum(jnp.abs(den), jnp.exp(-m_t))
            y = _rms(hh, mg_ref[:, h * dh:(h + 1) * dh]) * _sigmoid(o.astype(F32))
            yb_ref[pl.ds(r0, CHUNK), h * dh:(h + 1) * dh] = y.astype(BF16)

            g = b_col[CHUNK - 1:CHUNK, :]
            log_w = g - b_col + i_col
            m_new = jnp.maximum(g + m_prev, jnp.max(log_w, axis=0, keepdims=True))
            w_state = jnp.exp(log_w - m_new) * scale
            decay = jnp.exp(g + m_prev - m_new)
            wv = (v_aug.astype(F32) * w_state).astype(BF16)
            upd = _dot(k.astype(F32).T.astype(BF16), wv)
            c_s[st] = decay * cst + upd
            m_s[st:st + 1, 0:1] = m_new

    lax.fori_loop(0, nc, step, 0)


def _mlstm(qkvo, gate, grow, mnorm_g, batch, seq, heads, dh):
    t = qkvo.shape[0]
    width = heads * dh
    nc = seq // CHUNK
    nseq = MLSTM_SEQS if batch % MLSTM_SEQS == 0 else 1
    assert nseq * heads <= SUBLANES
    rows = nseq * seq
    return pl.pallas_call(
        functools.partial(_mlstm_body, heads=heads, dh=dh, nseq=nseq),
        grid=(batch // nseq,),
        in_specs=[
            pl.BlockSpec((rows, 4 * width), lambda b: (b, 0)),
            pl.BlockSpec((rows, LANES), lambda b: (b, 0)),
            pl.BlockSpec((nseq * nc, SUBLANES, CHUNK), lambda b: (b, 0, 0)),
            pl.BlockSpec((1, width), lambda b: (0, 0)),
        ],
        out_specs=pl.BlockSpec((rows, width), lambda b: (b, 0)),
        out_shape=jax.ShapeDtypeStruct((t, width), BF16),
        scratch_shapes=[
            pltpu.VMEM((nseq * heads, dh, 2 * dh), F32),
            pltpu.VMEM((SUBLANES, LANES), F32),
        ],
        compiler_params=_cparams(("parallel",)),
        name="mlstm",
    )(qkvo, gate, grow, mnorm_g)


def _route(logits_t):
    tm = logits_t.shape[1]
    row = lax.broadcasted_iota(I32, (SUBLANES, tm), 0)
    rowf = row.astype(F32)
    none = float(SUBLANES)
    gl = jnp.where(row < N_GROUPS, logits_t[0:SUBLANES], NEG_INF)
    gmax = jnp.max(gl, axis=0, keepdims=True)
    gidx = jnp.min(jnp.where(gl == gmax, rowf, none), axis=0, keepdims=True)
    gweight = 1.0 / jnp.sum(jnp.exp(gl - gmax), axis=0, keepdims=True)
    el = logits_t[SUBLANES:2 * SUBLANES]
    for g in range(1, N_GROUPS):
        el = jnp.where(gidx == float(g), logits_t[(g + 1) * SUBLANES:(g + 2) * SUBLANES], el)
    v1 = jnp.max(el, axis=0, keepdims=True)
    i1 = jnp.min(jnp.where(el == v1, rowf, none), axis=0, keepdims=True)
    el2 = jnp.where(rowf == i1, NEG_INF, el)
    v2 = jnp.max(el2, axis=0, keepdims=True)
    i2 = jnp.min(jnp.where(el2 == v2, rowf, none), axis=0, keepdims=True)
    t = jnp.exp(v2 - v1)
    w1 = gweight / (1.0 + t)
    w2 = gweight * t / (1.0 + t)
    e1 = (gidx * EXPERTS_PER_GROUP + i1).astype(I32)
    e2 = (gidx * EXPERTS_PER_GROUP + i2).astype(I32)
    ids = jnp.where(row == 0, e1, jnp.where(row == 1, e2, 0))
    wts = jnp.where(row == 0, w1, jnp.where(row == 1, w2, 0.0))
    return ids, wts


def _outproj_router_body(*refs, nact):
    h_ref = refs[0]
    act_refs = refs[1:1 + nact]
    w_refs = refs[1 + nact:1 + 2 * nact]
    g_ref, wr_ref, br_ref, hn_ref, xn_ref, ri_ref, rw_ref = refs[1 + 2 * nact:]
    acc = h_ref[...]
    for a_ref, w_ref in zip(act_refs, w_refs):
        acc = acc + _dot(a_ref[...], w_ref[...])
    hn_ref[...] = acc
    xn = _rms(acc, g_ref[...])
    _store_slabs(xn_ref, xn)
    xh = xn.astype(BF16)
    xl = (xn - xh.astype(F32)).astype(BF16)
    part = _dot(xh, wr_ref[...])
    logits = part[:, 0:LANES] + part[:, LANES:2 * LANES] + _dot(xl, wr_ref[:, 0:LANES]) + br_ref[...]
    ids, wts = _route(logits.T)
    ri_ref[...] = ids
    rw_ref[...] = wts


def _outproj_router(h, acts, ws, g, wr, br):
    t, d = h.shape
    nact = len(acts)
    row_spec = pl.BlockSpec((TM, d), lambda i: (i, 0))
    in_specs = [row_spec]
    in_specs += [pl.BlockSpec((TM, a.shape[1]), lambda i: (i, 0)) for a in acts]
    in_specs += [pl.BlockSpec(w.shape, lambda i: (0, 0)) for w in ws]
    in_specs += [
        pl.BlockSpec((1, d), lambda i: (0, 0)),
        pl.BlockSpec((d, 2 * LANES), lambda i: (0, 0)),
        pl.BlockSpec((1, LANES), lambda i: (0, 0)),
    ]
    return pl.pallas_call(
        functools.partial(_outproj_router_body, nact=nact),
        grid=(t // TM,),
        in_specs=in_specs,
        out_specs=[
            row_spec,
            pl.BlockSpec((TM * _slab_rows(d), LANES), lambda i: (i, 0)),
            pl.BlockSpec((SUBLANES, TM), lambda i: (0, i)),
            pl.BlockSpec((SUBLANES, TM), lambda i: (0, i)),
        ],
        out_shape=[
            jax.ShapeDtypeStruct((t, d), F32),
            jax.ShapeDtypeStruct((t * _slab_rows(d), LANES), F32),
            jax.ShapeDtypeStruct((SUBLANES, t), I32),
            jax.ShapeDtypeStruct((SUBLANES, t), F32),
        ],
        compiler_params=_cparams(("parallel",)),
        name="outproj_router",
    )(h, *acts, *ws, g, wr, br)


def _moe_body(te_ref, nt_ref, dst_ref, x_ref, wup_ref, wdn_ref, y_hbm, dst_s, ybuf, sem_i, sem_s):
    i = pl.program_id(0)
    nt = nt_ref[0]
    tm = dst_ref.shape[2]
    d, ff2 = wup_ref.shape
    ff = ff2 // 2
    sr = _slab_rows(d)
    slot = i % 2
    other = 1 - slot
    n_pieces = 2 * MOE_COL_CHUNKS
    rows_per_piece = tm // n_pieces

    def scatter(r, s):
        return pltpu.make_async_copy(ybuf.at[s, pl.ds(r * sr, sr), :], y_hbm.at[pl.ds(dst_s[0, 0, r], sr), :],
                                     sem_s.at[0])

    def wait_all(copy, s):
        def body(r, c):
            copy(r, s).wait()
            return c
        lax.fori_loop(0, tm, body, 0, unroll=8)

    @pl.when(i == 0)
    def _():
        ybuf[...] = jnp.zeros_like(ybuf)
        pad = pltpu.make_async_copy(ybuf.at[0], y_hbm.at[pl.ds(y_hbm.shape[0] - tm * sr, tm * sr), :], sem_s.at[0])
        pad.start()
        pad.wait()

    @pl.when(i <= nt)
    def _():
        @pl.when(i >= 1)
        def _():
            wait_all(scatter, slot)

        cd = pltpu.make_async_copy(dst_ref, dst_s, sem_i.at[0])
        cd.start()
        cd.wait()

        def issue(piece):
            for r in range(piece * rows_per_piece, (piece + 1) * rows_per_piece):
                scatter(r, other).start(priority=r % 2)

        x = jnp.concatenate([p.astype(BF16) for p in _load_slabs(x_ref, tm, d)], axis=1)
        cw = ff2 // MOE_COL_CHUNKS
        parts = []
        for c in range(MOE_COL_CHUNKS):
            issue(c)
            parts.append(_dot(x, wup_ref[:, c * cw:(c + 1) * cw].astype(BF16)))
        hcat = jnp.concatenate(parts, axis=1)
        hh = (jax.nn.silu(hcat[:, 0:ff]) * hcat[:, ff:ff2]).astype(BF16)
        cw = d // MOE_COL_CHUNKS
        parts = []
        for c in range(MOE_COL_CHUNKS):
            issue(MOE_COL_CHUNKS + c)
            parts.append(_dot(hh, wdn_ref[:, c * cw:(c + 1) * cw].astype(BF16)))
        _store_slabs(ybuf, jnp.concatenate(parts, axis=1), lead=(slot,))

        @pl.when(i == nt)
        def _():
            wait_all(scatter, other)


def _moe(xs_slabs, tile_expert, num_tiles, dst, w_up, w_down, layer, n_rows_out):
    d, ff2 = w_up.shape[2], w_up.shape[3]
    nrows, _, tm = dst.shape
    sr = _slab_rows(d)
    steps = tile_expert.shape[0]
    assert nrows == steps and tm % (2 * MOE_COL_CHUNKS) == 0

    grid_spec = pltpu.PrefetchScalarGridSpec(
        num_scalar_prefetch=2,
        grid=(steps,),
        in_specs=[
            pl.BlockSpec((1, 1, tm), lambda i, te, nt: (i, 0, 0)),
            pl.BlockSpec((tm * sr, LANES), lambda i, te, nt: (jnp.minimum(i, nt[0] - 1), 0)),
            pl.BlockSpec((None, None, d, ff2), lambda i, te, nt: (layer, te[i], 0, 0)),
            pl.BlockSpec((None, None, ff2 // 2, d), lambda i, te, nt: (layer, te[i], 0, 0)),
        ],
        out_specs=pl.BlockSpec(memory_space=pl.ANY),
        scratch_shapes=[
            pltpu.SMEM((1, 1, tm), I32),
            pltpu.VMEM((2, tm * sr, LANES), F32),
            pltpu.SemaphoreType.DMA((1,)),
            pltpu.SemaphoreType.DMA((1,)),
        ],
    )
    return pl.pallas_call(
        _moe_body,
        grid_spec=grid_spec,
        out_shape=jax.ShapeDtypeStruct((n_rows_out * sr, LANES), F32),
        compiler_params=_cparams(("arbitrary",)),
        name="moe_experts",
    )(tile_expert, num_tiles, dst, xs_slabs, w_up, w_down)


def _scatter_rows_body(zf_ref, pos_ref, x_ref, xs_hbm, pos_s, zbuf, sem_i, sem_s, sem_z):
    m = pos_ref.shape[1]
    sr = x_ref.shape[0] // m

    @pl.when(pl.program_id(0) == 0)
    def _():
        zbuf[...] = jnp.zeros_like(zbuf)
        rows = zbuf.shape[0]

        def zero_tile(j):
            return pltpu.make_async_copy(zbuf, xs_hbm.at[pl.ds(j * rows, rows), :], sem_z.at[0])

        for j in range(zf_ref.shape[0]):
            @pl.when(zf_ref[j] == 1)
            def _():
                zero_tile(j).start()
        for j in range(zf_ref.shape[0]):
            @pl.when(zf_ref[j] == 1)
            def _():
                zero_tile(j).wait()

    cp = pltpu.make_async_copy(pos_ref, pos_s, sem_i.at[0])
    cp.start()
    cp.wait()

    def put(k, r):
        return pltpu.make_async_copy(x_ref.at[pl.ds(r * sr, sr), :], xs_hbm.at[pl.ds(pos_s[k, r], sr), :], sem_s.at[0])

    for r in range(m):
        for k in range(TOP_K):
            put(k, r).start(priority=r % 2)

    def drain(r, c):
        for k in range(TOP_K):
            put(k, r).wait()
        return c
    lax.fori_loop(0, m, drain, 0, unroll=8)


def _scatter_rows(xn_slabs, pos, zero_flags):
    t = pos.shape[1]
    sr = xn_slabs.shape[0] // t
    n_rows_out = zero_flags.shape[0] * TM_MOE
    grid_spec = pltpu.PrefetchScalarGridSpec(
        num_scalar_prefetch=1,
        grid=(t // TM,),
        in_specs=[
            pl.BlockSpec((TOP_K, TM), lambda i, zf: (0, i)),
            pl.BlockSpec((TM * sr, LANES), lambda i, zf: (i, 0)),
        ],
        out_specs=pl.BlockSpec(memory_space=pl.ANY),
        scratch_shapes=[
            pltpu.SMEM((TOP_K, TM), I32),
            pltpu.VMEM((TM_MOE * sr, LANES), F32),
            pltpu.SemaphoreType.DMA((1,)),
            pltpu.SemaphoreType.DMA((1,)),
            pltpu.SemaphoreType.DMA((1,)),
        ],
    )
    return pl.pallas_call(
        _scatter_rows_body,
        grid_spec=grid_spec,
        out_shape=jax.ShapeDtypeStruct((n_rows_out * sr, LANES), F32),
        compiler_params=_cparams(("arbitrary",)),
        name="moe_dispatch",
    )(zero_flags, pos, xn_slabs)


def _dispatch(ri, t, sr):
    tm = TM_MOE
    n_assign = TOP_K * t
    ntiles = n_assign // tm + N_EXPERTS
    e_flat = ri[0:TOP_K].reshape(n_assign)
    e_sorted, a_sorted = lax.sort((e_flat, jnp.arange(n_assign, dtype=I32)), num_keys=1)
    bounds = jnp.sum((e_sorted[None, :] < jnp.arange(N_EXPERTS + 1, dtype=I32)[:, None]).astype(I32), axis=1)
    start = bounds[:-1]
    counts = bounds[1:] - start
    tiles_per = (counts + tm - 1) // tm
    tile_end = jnp.cumsum(tiles_per)
    num_tiles = tile_end[-1:].astype(I32)
    tile_ids = jnp.arange(ntiles, dtype=I32)
    tile_expert = jnp.minimum(jnp.sum((tile_ids[:, None] >= tile_end[None, :]).astype(I32), axis=1), N_EXPERTS - 1)
    last_valid = jnp.take(tile_expert, jnp.maximum(num_tiles[0] - 1, 0))
    tile_expert = jnp.where(tile_ids < num_tiles[0], tile_expert, last_valid).astype(I32)
    rows_before = (tile_ids - jnp.take(tile_end - tiles_per, tile_expert)) * tm
    first = jnp.take(start, tile_expert) + rows_before
    n_valid = jnp.where(tile_ids < num_tiles[0], jnp.clip(jnp.take(counts, tile_expert) - rows_before, 0, tm), 0)
    r = jnp.arange(tm, dtype=I32)
    valid = r[None, :] < n_valid[:, None]
    idx = jnp.clip(first[:, None] + r[None, :], 0, n_assign - 1)
    a = jnp.take(a_sorted, idx)
    dst = (jnp.where(valid, a, n_assign + r[None, :]) * sr).astype(I32).reshape(ntiles, 1, tm)
    pad_dst = ((n_assign + jnp.arange(tm, dtype=I32)) * sr).reshape(1, 1, tm)
    dst = jnp.concatenate([pad_dst, dst], axis=0)
    tile_expert = jnp.concatenate([tile_expert, jnp.broadcast_to(last_valid, (1,)).astype(I32)])
    shift = (tile_end - tiles_per) * tm - start
    onehot = e_sorted[None, :] == jnp.arange(N_EXPERTS, dtype=I32)[:, None]
    pos_sorted = jnp.arange(n_assign, dtype=I32) + jnp.sum(jnp.where(onehot, shift[:, None], 0), axis=0)
    _, pos = lax.sort((a_sorted, pos_sorted), num_keys=1)
    pos = (pos * sr).astype(I32).reshape(TOP_K, t)
    is_last = jnp.any((tile_ids[:, None] == (tile_end - 1)[None, :]) & (tiles_per > 0)[None, :], axis=1)
    zero_flags = (is_last | (tile_ids >= num_tiles[0])).astype(I32)
    return tile_expert, num_tiles, dst, pos, zero_flags


def _combine(h_ref, y0_ref, y1_ref, rw_ref):
    m, d = h_ref.shape
    rw = rw_ref[...]
    gates = jnp.concatenate([rw, jnp.zeros((LANES - SUBLANES, m), F32)], axis=0).T
    y0 = jnp.concatenate(_load_slabs(y0_ref, m, d), axis=1)
    y1 = jnp.concatenate(_load_slabs(y1_ref, m, d), axis=1)
    return h_ref[...] + gates[:, 0:1] * y0 + gates[:, 1:2] * y1


def _slab_specs(d, nb):
    rows = TM * _slab_rows(d)
    return [pl.BlockSpec((rows, LANES), lambda i: (i, 0)), pl.BlockSpec((rows, LANES), lambda i: (i + nb, 0)),
            pl.BlockSpec((SUBLANES, TM), lambda i: (0, i))]


def _inproj1_body(h_ref, y0_ref, y1_ref, rw_ref, g_ref, w_ref, hn_ref, qkv_ref):
    h = _combine(h_ref, y0_ref, y1_ref, rw_ref)
    d = h.shape[1]
    hn_ref[...] = h
    xb = _rms(h, g_ref[...]).astype(BF16)
    for c in range(w_ref.shape[1] // d):
        qkv_ref[:, c * d:(c + 1) * d] = _dot(xb, w_ref[:, c * d:(c + 1) * d]).astype(BF16)


def _inproj1(h, y2, rw, g, w):
    t, d = h.shape
    n = w.shape[1]
    nb = t // TM
    row_spec = pl.BlockSpec((TM, d), lambda i: (i, 0))
    return pl.pallas_call(
        _inproj1_body,
        grid=(nb,),
        in_specs=[row_spec] + _slab_specs(d, nb) + [
            pl.BlockSpec((1, d), lambda i: (0, 0)),
            pl.BlockSpec((d, n), lambda i: (0, 0)),
        ],
        out_specs=[row_spec, pl.BlockSpec((TM, n), lambda i: (i, 0))],
        out_shape=[jax.ShapeDtypeStruct((t, d), F32), jax.ShapeDtypeStruct((t, n), BF16)],
        compiler_params=_cparams(("parallel",)),
        name="inproj1",
    )(h, y2, y2, rw, g, w)


def _attn_body(q_ref, k_ref, v_ref, lq1_ref, lk1_ref, lq2_ref, lk2_ref, ng_ref, o_ref,
               vaug_s, bias_s, s_s, p_s, m_s, *, heads, lambda_init):
    h = pl.program_id(1)
    seq, dv = q_ref.shape
    nq = seq // TQ
    dqk = dv // 2
    slope = jnp.exp2(jnp.full((1, 1), -8.0 / heads, F32) * (h + 1).astype(F32))
    scale = dqk ** -0.5

    vaug_s[:, 0:dv] = v_ref[...]
    vaug_s[:, dv:2 * dv] = (lax.broadcasted_iota(I32, (seq, dv), 1) == 0).astype(BF16)

    rowp = lax.broadcasted_iota(I32, (2 * TQ, TK), 0) % TQ
    colp = lax.broadcasted_iota(I32, (2 * TQ, TK), 1)
    allowed = (colp // CHUNK) <= (rowp // CHUNK)
    rowf = rowp.astype(F32)
    bias_s[...] = jnp.where(allowed, slope * (rowf - jnp.abs(rowf - colp.astype(F32))), NEG_INF)
    kcol = slope * lax.broadcasted_iota(I32, (1, TK), 1).astype(F32)
    lam = (jnp.exp(jnp.sum(lq1_ref[...] * lk1_ref[...], axis=-1, keepdims=True))
           - jnp.exp(jnp.sum(lq2_ref[...] * lk2_ref[...], axis=-1, keepdims=True)) + lambda_init)
    gain = ng_ref[pl.ds(h, 1), :] * (1.0 - lambda_init)
    lane = lax.broadcasted_iota(I32, (TQ, dv), 1)

    for qi in range(nq):
        q = q_ref[qi * TQ:(qi + 1) * TQ, :].astype(F32) * scale
        qs = jnp.concatenate([jnp.where(lane < dqk, q, 0.0), jnp.where(lane >= dqk, q, 0.0)], axis=0).astype(BF16)
        mlane = None
        for j in range(qi + 1):
            s = _dot_nt(qs, k_ref[j * TK:(j + 1) * TK, :])
            if j < qi:
                s = s + (kcol - slope * float((qi - j) * TK))
            else:
                s = s + bias_s[...]
            s_s[:, j * TK:(j + 1) * TK] = s
            for c in range(TK // LANES):
                part = s[:, c * LANES:(c + 1) * LANES]
                mlane = part if mlane is None else jnp.maximum(mlane, part)
        m_s[...] = jnp.broadcast_to(jnp.max(mlane, axis=-1, keepdims=True), m_s.shape)
        for j in range(qi + 1):
            for c in range(TK // LANES):
                lo = j * TK + c * LANES
                p_s[:, lo:lo + LANES] = jnp.exp(s_s[:, lo:lo + LANES] - m_s[...]).astype(BF16)
        kv = (qi + 1) * TK
        acc = _dot(p_s[:, 0:kv], vaug_s[0:kv, :])
        o1 = acc[0:TQ, 0:dv] / acc[0:TQ, dv:dv + 1]
        o2 = acc[TQ:2 * TQ, 0:dv] / acc[TQ:2 * TQ, dv:dv + 1]
        o_ref[qi * TQ:(qi + 1) * TQ, :] = _rms(o1 - lam * o2, gain).astype(BF16)


def _attn(qkv, lq1, lk1, lq2, lk2, norm_g, batch, seq, heads, dv, lambda_init):
    t = qkv.shape[0]
    dqk = dv // 2
    small = pl.BlockSpec((1, dqk), lambda b, h: (0, 0))
    return pl.pallas_call(
        functools.partial(_attn_body, heads=heads, lambda_init=lambda_init),
        grid=(batch, heads),
        in_specs=[
            pl.BlockSpec((seq, dv), lambda b, h: (b, h)),
            pl.BlockSpec((seq, dv), lambda b, h: (b, heads + h)),
            pl.BlockSpec((seq, dv), lambda b, h: (b, 2 * heads + h)),
            small, small, small, small,
            pl.BlockSpec((heads, dv), lambda b, h: (0, 0)),
        ],
        out_specs=pl.BlockSpec((seq, dv), lambda b, h: (b, h)),
        out_shape=jax.ShapeDtypeStruct((t, heads * dv), BF16),
        scratch_shapes=[
            pltpu.VMEM((seq, 2 * dv), BF16),
            pltpu.VMEM((2 * TQ, TK), F32),
            pltpu.VMEM((2 * TQ, seq), F32),
            pltpu.VMEM((2 * TQ, seq), BF16),
            pltpu.VMEM((2 * TQ, LANES), F32),
        ],
        compiler_params=_cparams(("parallel", "parallel")),
        name="diff_attn",
    )(qkv, qkv, qkv, lq1, lk1, lq2, lk2, norm_g)


def _final_body(h_ref, y0_ref, y1_ref, rw_ref, g_ref, o_ref):
    o_ref[...] = _rms(_combine(h_ref, y0_ref, y1_ref, rw_ref), g_ref[...])


def _final(h, y2, rw, g):
    t, d = h.shape
    nb = t // TM
    row_spec = pl.BlockSpec((TM, d), lambda i: (i, 0))
    return pl.pallas_call(
        _final_body,
        grid=(nb,),
        in_specs=[row_spec] + _slab_specs(d, nb) + [pl.BlockSpec((1, d), lambda i: (0, 0))],
        out_specs=row_spec,
        out_shape=jax.ShapeDtypeStruct((t, d), F32),
        compiler_params=_cparams(("parallel",)),
        name="final_norm",
    )(h, y2, y2, rw, g)


def _block_diag(w):
    nb, bd, _ = w.shape
    eye = jnp.eye(nb, dtype=w.dtype)
    return (eye[:, None, :, None] * w[:, :, None, :]).reshape(nb * bd, nb * bd)


def _router_weights(w_group, b_group, w_expert, b_expert):
    d = w_group.shape[0]
    wr = jnp.zeros((d, LANES), F32)
    wr = wr.at[:, 0:N_GROUPS].set(w_group).at[:, SUBLANES:SUBLANES + N_EXPERTS].set(w_expert)
    br = jnp.zeros((1, LANES), F32)
    br = br.at[0, 0:N_GROUPS].set(b_group).at[0, SUBLANES:SUBLANES + N_EXPERTS].set(b_expert)
    wr_hi = wr.astype(BF16)
    wr_lo = (wr - wr_hi.astype(F32)).astype(BF16)
    return jnp.concatenate([wr_hi, wr_lo], axis=1), br


def _moe_layer(xn_slabs, ri, w_up, w_down, layer):
    t = ri.shape[1]
    tile_expert, num_tiles, dst, pos, zero_flags = _dispatch(ri, t, _slab_rows(w_up.shape[2]))
    xs_slabs = _scatter_rows(xn_slabs, pos, zero_flags)
    return _moe(xs_slabs, tile_expert, num_tiles, dst, w_up, w_down, layer, TOP_K * t + TM_MOE)


def kernel(x, norm_mix_g, norm_ffn_g, ab_w_in, ab_w_out, lru_conv_w, lru_conv_b, lru_w_r, lru_b_r, lru_w_i,
           lru_b_i, lru_lambda, mlstm_b_i, mlstm_b_f, mlstm_norm_g, diff_w_qkv, diff_w_o, diff_lambda_q1,
           diff_lambda_k1, diff_lambda_q2, diff_lambda_k2, diff_norm_g, moe_w_group, moe_b_group, moe_w_expert,
           moe_b_expert, moe_w_up, moe_w_down, final_norm_g):
    batch, seq, d = x.shape
    t = batch * seq
    lru_width = lru_lambda.shape[1]
    m_heads, m_dh = mlstm_norm_g.shape[1], mlstm_norm_g.shape[2]
    m_width = m_heads * m_dh
    a_heads, a_dv = diff_norm_g.shape[1], diff_norm_g.shape[2]
    n_main = 2 * lru_width + 4 * m_width
    assert n_main + 2 * m_heads == ab_w_in.shape[2]
    assert seq % TS_LRU == 0 and seq % TQ == 0 and t % TM == 0 and (TOP_K * t) % TM_MOE == 0
    assert 2 * m_heads <= SUBLANES and TQ == TK and TQ % CHUNK == 0

    x2 = x.reshape(t, d)

    w_in = ab_w_in[0]
    w_gate = jnp.zeros((d, LANES), F32).at[:, 0:2 * m_heads].set(w_in[:, n_main:]).astype(BF16)
    b_gate = jnp.zeros((1, LANES), F32).at[0, 0:m_heads].set(mlstm_b_i[0]).at[0, m_heads:2 * m_heads].set(mlstm_b_f[0])
    pf, qkvo, gate = _inproj0(x2, norm_mix_g[0:1], w_in[:, 0:n_main].astype(BF16), w_gate, b_gate)

    w_ri = jnp.concatenate([_block_diag(lru_w_r[0]), _block_diag(lru_w_i[0])], axis=1).astype(BF16)
    b_ri = jnp.concatenate([lru_b_r[0], lru_b_i[0]])[None, :]
    ya = _lru(pf, lru_conv_w[0], lru_conv_b[0][None, :], w_ri, b_ri, lru_lambda[0][None, :], batch, seq)

    grow = gate[:, 0:SUBLANES].reshape(t // CHUNK, CHUNK, SUBLANES).transpose(0, 2, 1)
    yb = _mlstm(qkvo, gate, grow, mlstm_norm_g[0].reshape(1, m_width), batch, seq, m_heads, m_dh)

    w_out = ab_w_out[0].astype(BF16)
    wr, br = _router_weights(moe_w_group[0], moe_b_group[0], moe_w_expert[0], moe_b_expert[0])
    h1, xn, ri, rw = _outproj_router(x2, [ya, yb], [w_out[0:lru_width], w_out[lru_width:]], norm_ffn_g[0:1], wr, br)
    y2 = _moe_layer(xn, ri, moe_w_up, moe_w_down, 0)

    lambda_init = 0.8 - 0.6 * math.exp(-0.3 * 1)
    h2, qkv = _inproj1(h1, y2, rw, norm_mix_g[1:2], diff_w_qkv[0].astype(BF16))
    attn = _attn(qkv, diff_lambda_q1[0][None, :], diff_lambda_k1[0][None, :], diff_lambda_q2[0][None, :],
                 diff_lambda_k2[0][None, :], diff_norm_g[0], batch, seq, a_heads, a_dv, lambda_init)
    wr, br = _router_weights(moe_w_group[1], moe_b_group[1], moe_w_expert[1], moe_b_expert[1])
    h3, xn, ri, rw = _outproj_router(h2, [attn], [diff_w_o[0].astype(BF16)], norm_ffn_g[1:2], wr, br)
    y2 = _moe_layer(xn, ri, moe_w_up, moe_w_down, 1)

    return _final(h3, y2, rw, final_norm_g[None, :]).reshape(batch, seq, d)
```
